```python
import jax, jax.numpy as jnp
from jax import lax
import numpy as np

D_MODEL = 1024
BATCH = 2
SEQ = 8192
DEPTH = 1

N_ATTN_HEADS = 8
ATTN_HEAD_DIM = 64
ATTN_WIDTH = N_ATTN_HEADS * ATTN_HEAD_DIM
MOBA_BLOCK = 256
MOBA_TOPK = 3
Q_BLOCK = 128
GMLP_GROUPS = 8
GMLP_GROUP_DIM = 64
GMLP_WIDTH = GMLP_GROUPS * GMLP_GROUP_DIM
GMLP_CHUNK = 128
IN_SPLITS = (ATTN_WIDTH, 2 * ATTN_WIDTH, 3 * ATTN_WIDTH,
             3 * ATTN_WIDTH + GMLP_WIDTH, 3 * ATTN_WIDTH + 2 * GMLP_WIDTH,
             3 * ATTN_WIDTH + 2 * GMLP_WIDTH + D_MODEL)
IN_COLS = 3 * ATTN_WIDTH + 2 * GMLP_WIDTH + 2 * D_MODEL
PEER_HEADS = 8
PEER_N_KEYS = 128
PEER_N_EXPERTS = PEER_N_KEYS * PEER_N_KEYS
PEER_QUERY_DIM = 256
PEER_HALF = PEER_QUERY_DIM // 2
PEER_TOPK = 16
PEER_TOKEN_BLOCK = 128
ALPHA = float((2.0 * DEPTH) ** 0.25)
BETA = float((8.0 * DEPTH) ** -0.25)
LN_EPS = 1e-5

kernel_name = 'moba_gmlp_peer_deepnorm_adaln_block'


def _layernorm(x):
    xf = x.astype(jnp.float32)
    mu = xf.mean(-1, keepdims=True)
    var = jnp.square(xf - mu).mean(-1, keepdims=True)
    return ((xf - mu) * lax.rsqrt(var + LN_EPS)).astype(x.dtype)


def _alibi_slopes(n):
    return jnp.asarray([2.0 ** (-8.0 * (i + 1) / n) for i in range(n)], jnp.float32)


def _moba_attention(q, k, v):
    B, S, H, Dh = q.shape
    nb = -(-S // MOBA_BLOCK)
    pad = nb * MOBA_BLOCK - S
    kp = jnp.pad(k, ((0, 0), (0, pad), (0, 0), (0, 0)))
    vp = jnp.pad(v, ((0, 0), (0, pad), (0, 0), (0, 0)))
    kb = kp.reshape(B, nb, MOBA_BLOCK, H, Dh).transpose(0, 3, 1, 2, 4)
    vb = vp.reshape(B, nb, MOBA_BLOCK, H, Dh).transpose(0, 3, 1, 2, 4)
    kmean = kb.astype(jnp.float32).mean(axis=3)
    k_sel = min(MOBA_TOPK, nb)
    nq = S // Q_BLOCK
    qc = q.reshape(B, nq, Q_BLOCK, H, Dh).transpose(1, 0, 3, 2, 4)
    slopes = _alibi_slopes(H)
    scale = Dh ** -0.5
    gather_bh = jax.vmap(jax.vmap(lambda tab, idx: tab[idx]))

    def one_block(args):
        q_blk, ci = args
        t = ci * Q_BLOCK + jnp.arange(Q_BLOCK)
        own = (ci * Q_BLOCK) // MOBA_BLOCK
        gate = jnp.einsum('bhqd,bhnd->bhqn', q_blk.astype(jnp.float32), kmean)
        gate = jnp.where(jnp.arange(nb) < own, gate, -jnp.inf)
        _, sel = lax.top_k(gate, k_sel)
        sel_ok = sel < own
        k_g = gather_bh(kb, sel)
        v_g = gather_bh(vb, sel)
        s_sel = jnp.einsum('bhqd,bhqkld->bhqkl', q_blk, k_g).astype(jnp.float32) * scale
        pos_sel = sel[..., None] * MOBA_BLOCK + jnp.arange(MOBA_BLOCK)
        dist_sel = jnp.abs(t[None, None, :, None, None] - pos_sel).astype(jnp.float32)
        s_sel = s_sel - slopes[None, :, None, None, None] * dist_sel
        s_sel = jnp.where(sel_ok[..., None], s_sel, -jnp.inf)
        k_own = lax.dynamic_index_in_dim(kb, own, axis=2, keepdims=False)
        v_own = lax.dynamic_index_in_dim(vb, own, axis=2, keepdims=False)
        s_own = jnp.einsum('bhqd,bhld->bhql', q_blk, k_own).astype(jnp.float32) * scale
        diff = t[:, None] - (own * MOBA_BLOCK + jnp.arange(MOBA_BLOCK))[None, :]
        s_own = s_own - slopes[:, None, None] * jnp.abs(diff).astype(jnp.float32)
        s_own = jnp.where(diff >= 0, s_own, -jnp.inf)
        scores = jnp.concatenate([s_sel.reshape(B, H, Q_BLOCK, k_sel * MOBA_BLOCK), s_own], axis=-1)
        p = jax.nn.softmax(scores, axis=-1)
        p_sel = p[..., :k_sel * MOBA_BLOCK].reshape(B, H, Q_BLOCK, k_sel, MOBA_BLOCK).astype(v.dtype)
        p_own = p[..., k_sel * MOBA_BLOCK:].astype(v.dtype)
        return (jnp.einsum('bhqkl,bhqkld->bhqd', p_sel, v_g)
                + jnp.einsum('bhql,bhld->bhqd', p_own, v_own))

    out = lax.map(one_block, (qc, jnp.arange(nq)))
    return out.transpose(1, 0, 3, 2, 4).reshape(B, S, H * Dh)


def _chunked_sgu(gu, gv, sgu_ln_g, sgu_ln_b, w_spatial, b_spatial):
    B, S, _ = gu.shape
    u = jax.nn.gelu(gu, approximate=False)
    vv = jax.nn.gelu(gv, approximate=False).reshape(B, S, GMLP_GROUPS, GMLP_GROUP_DIM)
    vv = _layernorm(vv) * sgu_ln_g + sgu_ln_b
    nc = S // GMLP_CHUNK
    vv = vv.reshape(B, nc, GMLP_CHUNK, GMLP_GROUPS, GMLP_GROUP_DIM)
    mask = jnp.tril(jnp.ones((GMLP_CHUNK, GMLP_CHUNK), dtype=bool))
    w = jnp.where(mask, w_spatial, 0)
    sv = jnp.einsum('gts,bcsgd->bctgd', w, vv) + b_spatial.T[None, None, :, :, None]
    return u * sv.reshape(B, S, GMLP_WIDTH)


def _peer(h, w_peer_q, peer_sub_keys, peer_u, peer_v):
    B, S, D = h.shape
    q = (h @ w_peer_q).reshape(B, S, PEER_HEADS, 2, PEER_HALF)
    s = jnp.einsum('bshcd,hcnd->bshcn', q, peer_sub_keys).astype(jnp.float32)
    top_s, top_i = lax.top_k(s, PEER_TOPK)
    cand = (top_s[..., 0, :, None] + top_s[..., 1, None, :]).reshape(B, S, PEER_HEADS, PEER_TOPK * PEER_TOPK)
    cand_idx = (top_i[..., 0, :, None] * PEER_N_KEYS + top_i[..., 1, None, :]).reshape(B, S, PEER_HEADS, PEER_TOPK * PEER_TOPK)
    best_s, best_pos = lax.top_k(cand, PEER_TOPK)
    expert_idx = jnp.take_along_axis(cand_idx, best_pos, axis=-1)
    g = jax.nn.softmax(best_s, axis=-1).astype(h.dtype)
    nt = (B * S) // PEER_TOKEN_BLOCK
    e_per_tok = PEER_HEADS * PEER_TOPK
    hx = h.reshape(nt, PEER_TOKEN_BLOCK, D)
    ix = expert_idx.reshape(nt, PEER_TOKEN_BLOCK, e_per_tok)
    gx = g.reshape(nt, PEER_TOKEN_BLOCK, e_per_tok)

    def one(args):
        hb, ib, gb = args
        u = peer_u[ib]
        a = jax.nn.gelu(jnp.einsum('td,ted->te', hb, u), approximate=False)
        return jnp.einsum('te,ted->td', gb * a, peer_v[ib])

    return lax.map(one, (hx, ix, gx)).reshape(B, S, D)


def _modulate(x, shift, scale):
    return _layernorm(x) * (1 + scale[:, None, :]) + shift[:, None, :]


def setup_inputs(seed: int = 0) -> dict:
    key = jax.random.key(seed)
    ks = jax.random.split(key, 24)
    L, D = DEPTH, D_MODEL
    nrm = lambda k, shape, s: jax.random.normal(k, shape, jnp.float32) * s
    return {
        'x': nrm(ks[0], (BATCH, SEQ, D), 1.0),
        'c': nrm(ks[1], (BATCH, D), 1.0),
        'w_cond': nrm(ks[2], (L, D, 6 * D), 0.2 * D ** -0.5),
        'b_cond': nrm(ks[3], (L, 6 * D), 0.01),
        'w_in': nrm(ks[4], (L, D, IN_COLS), D ** -0.5),
        'w_attn_up': nrm(ks[5], (L, ATTN_WIDTH, D), ATTN_WIDTH ** -0.5),
        'w_gmlp_up': nrm(ks[6], (L, GMLP_WIDTH, D), GMLP_WIDTH ** -0.5),
        'w_out': nrm(ks[7], (L, D, D), BETA * D ** -0.5),
        'sgu_ln_g': 1.0 + nrm(ks[8], (L, GMLP_GROUPS, GMLP_GROUP_DIM), 0.02),
        'sgu_ln_b': nrm(ks[9], (L, GMLP_GROUPS, GMLP_GROUP_DIM), 0.02),
        'w_spatial': nrm(ks[10], (L, GMLP_GROUPS, GMLP_CHUNK, GMLP_CHUNK), 0.5 * GMLP_CHUNK ** -0.5),
        'b_spatial': 1.0 + nrm(ks[11], (L, GMLP_GROUPS, GMLP_CHUNK), 0.02),
        'ln1_g': 1.0 + nrm(ks[12], (L, D), 0.02),
        'ln1_b': nrm(ks[13], (L, D), 0.02),
        'w_peer_q': nrm(ks[14], (L, D, PEER_HEADS * PEER_QUERY_DIM), D ** -0.5),
        'peer_sub_keys': nrm(ks[15], (L, PEER_HEADS, 2, PEER_N_KEYS, PEER_HALF), PEER_HALF ** -0.5),
        'peer_u': nrm(ks[16], (L, PEER_N_EXPERTS, D), D ** -0.5),
        'peer_v': nrm(ks[17], (L, PEER_N_EXPERTS, D), BETA),
        'ln2_g': 1.0 + nrm(ks[18], (L, D), 0.02),
        'ln2_b': nrm(ks[19], (L, D), 0.02),
    }


def reference(x, c, w_cond, b_cond, w_in, w_attn_up, w_gmlp_up, w_out, sgu_ln_g, sgu_ln_b,
              w_spatial, b_spatial, ln1_g, ln1_b, w_peer_q, peer_sub_keys, peer_u, peer_v,
              ln2_g, ln2_b):
    B, S, D = x.shape
    cs = jax.nn.silu(c)
    for l in range(DEPTH):
        mod = cs @ w_cond[l] + b_cond[l]
        sh1, sc1, gt1, sh2, sc2, gt2 = jnp.split(mod, 6, axis=-1)
        h = _modulate(x, sh1, sc1)
        z = h @ w_in[l]
        q, k, v, gu, gv, ga, gg = jnp.split(z, IN_SPLITS, axis=-1)
        hs = (B, S, N_ATTN_HEADS, ATTN_HEAD_DIM)
        attn = _moba_attention(q.reshape(hs), k.reshape(hs), v.reshape(hs))
        sgu = _chunked_sgu(gu, gv, sgu_ln_g[l], sgu_ln_b[l], w_spatial[l], b_spatial[l])
        merged = (jax.nn.sigmoid(ga) * (attn @ w_attn_up[l])
                  + jax.nn.sigmoid(gg) * (sgu @ w_gmlp_up[l]))
        out = merged @ w_out[l]
        x = _layernorm(ALPHA * x + (1 + gt1[:, None, :]) * out) * ln1_g[l] + ln1_b[l]
        h = _modulate(x, sh2, sc2)
        p = _peer(h, w_peer_q[l], peer_sub_keys[l], peer_u[l], peer_v[l])
        x = _layernorm(ALPHA * x + (1 + gt2[:, None, :]) * p) * ln2_g[l] + ln2_b[l]
    return x
```

```python
import functools

import jax
import jax.numpy as jnp
from jax import lax
from jax.experimental import pallas as pl
from jax.experimental.pallas import tpu as pltpu

F32 = jnp.float32
BF16 = jnp.bfloat16
HIGHEST = lax.Precision.HIGHEST

N_ATTN_HEADS = 8
ATTN_HEAD_DIM = 64
MOBA_BLOCK = 256
MOBA_TOPK = 3
GMLP_GROUPS = 8
GMLP_GROUP_DIM = 64
GMLP_CHUNK = 128
PEER_HEADS = 8
PEER_N_KEYS = 128
PEER_TOPK = 16
LN_EPS = 1e-5

LANES = 128
V7X_VMEM_LIMIT_BYTES = 56 * 1024 * 1024

NEG_BIG = -1e30
LOG2E = 1.4426950408889634


def _params(*sem):
  return pltpu.CompilerParams(dimension_semantics=sem, vmem_limit_bytes=V7X_VMEM_LIMIT_BYTES)


def _layernorm(x):
  mu = jnp.mean(x, axis=-1, keepdims=True)
  d = x - mu
  var = jnp.mean(d * d, axis=-1, keepdims=True)
  return d * lax.rsqrt(var + LN_EPS)


def _gelu(x):
  return 0.5 * x * (1.0 + lax.erf(x * (2.0 ** -0.5)))


def _cond_kernel(c_ref, w_ref, b_ref, o_ref):
  cs = jax.nn.silu(c_ref[...])
  o_ref[...] = jnp.dot(cs, w_ref[...], preferred_element_type=F32, precision=HIGHEST) + b_ref[...]


def _cond(c, w, b):
  bsz, d = c.shape
  n = w.shape[1]
  rows = 8
  cp = jnp.zeros((rows, d), F32).at[:bsz].set(c)
  tn = 768 if n % 768 == 0 else n
  out = pl.pallas_call(
      _cond_kernel,
      grid=(n // tn,),
      in_specs=[
          pl.BlockSpec((rows, d), lambda j: (0, 0)),
          pl.BlockSpec((d, tn), lambda j: (0, j)),
          pl.BlockSpec((1, tn), lambda j: (0, j)),
      ],
      out_specs=pl.BlockSpec((rows, tn), lambda j: (0, j)),
      out_shape=jax.ShapeDtypeStruct((rows, n), F32),
      compiler_params=_params("arbitrary"),
      name="adaln_cond",
  )(cp, w, b.reshape(1, n))
  return out[:bsz]


def _inproj_kernel(x_ref, sh_ref, sc_ref, wq, wk, wv, wgu, wgv, wga, wgg,
                   q_ref, k_ref, v_ref, km_ref, gu_ref, gv_ref, ga_ref, gg_ref):
  h = (_layernorm(x_ref[0]) * (1.0 + sc_ref[0]) + sh_ref[0]).astype(BF16)
  dot = lambda w: jnp.dot(h, w[...], preferred_element_type=F32)
  q_ref[0] = dot(wq)
  k = dot(wk)
  tm, width = k.shape
  km_ref[0, 0] = jnp.mean(k.reshape(tm // MOBA_BLOCK, MOBA_BLOCK, width), axis=1)
  k_ref[0] = k.astype(BF16)
  v_ref[0] = dot(wv).astype(BF16)
  gu_ref[0] = dot(wgu)
  gv_ref[0] = dot(wgv)
  ga_ref[0] = dot(wga)
  gg_ref[0] = dot(wgg)


def _in_proj(x, sh, sc, w_in):
  bsz, s, d = x.shape
  aw = N_ATTN_HEADS * ATTN_HEAD_DIM
  gw = GMLP_GROUPS * GMLP_GROUP_DIM
  bounds = [0, aw, 2 * aw, 3 * aw, 3 * aw + gw, 3 * aw + 2 * gw, 3 * aw + 2 * gw + d, 3 * aw + 2 * gw + 2 * d]
  assert w_in.shape[1] == bounds[-1]
  ws = [w_in[:, a:b].astype(BF16) for a, b in zip(bounds[:-1], bounds[1:])]
  tm = 512
  assert s % tm == 0 and tm % MOBA_BLOCK == 0
  nt = s // tm
  row = lambda width: pl.BlockSpec((1, tm, width), lambda b, i: (b, i, 0))
  vec = pl.BlockSpec((1, 1, d), lambda b, i: (b, 0, 0))
  wspec = lambda w: pl.BlockSpec(w.shape, lambda b, i: (0, 0))
  sds = lambda width, dt: jax.ShapeDtypeStruct((bsz, s, width), dt)
  kpt = tm // MOBA_BLOCK
  outs = pl.pallas_call(
      _inproj_kernel,
      grid=(bsz, nt),
      in_specs=[row(d), vec, vec] + [wspec(w) for w in ws],
      out_specs=[row(aw), row(aw), row(aw),
                 pl.BlockSpec((1, 1, kpt, aw), lambda b, i: (b, i, 0, 0)),
                 row(gw), row(gw), row(d), row(d)],
      out_shape=[sds(aw, F32), sds(aw, BF16), sds(aw, BF16),
                 jax.ShapeDtypeStruct((bsz, nt, kpt, aw), F32),
                 sds(gw, F32), sds(gw, F32), sds(d, F32), sds(d, F32)],
      compiler_params=_params("arbitrary", "arbitrary"),
      name="in_proj",
  )(x, sh, sc, *ws)
  q, k, v, km, gu, gv, ga, gg = outs
  return q, k, v, km.reshape(bsz, s // MOBA_BLOCK, aw), gu, gv, ga, gg


def _alibi_slopes():
  return [2.0 ** (-8.0 * (i + 1) / N_ATTN_HEADS) for i in range(N_ATTN_HEADS)]


def _key_side_table(s):
  nb = s // MOBA_BLOCK
  assert nb + 6 <= LANES
  pos = jnp.arange(s, dtype=F32)
  blk = jnp.arange(s, dtype=jnp.int32) // MOBA_BLOCK
  onehot = (blk[:, None] == jnp.arange(nb)[None, :]).astype(F32)
  slopes = _alibi_slopes()

  def top16(a):
    bits = lax.bitcast_convert_type(a, jnp.uint32) & jnp.uint32(0xFFFF0000)
    return lax.bitcast_convert_type(bits, F32)

  tables = []
  for hp in range(N_ATTN_HEADS // 2):
    cols = [onehot]
    for h in (2 * hp, 2 * hp + 1):
      c = (slopes[h] * LOG2E) * pos
      hi = top16(c)
      mid = top16(c - hi)
      lo = c - hi - mid
      cols += [hi[:, None], mid[:, None], lo[:, None]]
    t = jnp.concatenate(cols, axis=1)
    tables.append(jnp.pad(t, ((0, 0), (0, LANES - t.shape[1]))))
  return jnp.stack(tables).astype(BF16)


def _attn_kernel(q_ref, k_ref, v_ref, kb_ref, kmt_ref, o_ref, m_scr, l_scr, acc_scr, *, nb):
  j = pl.program_id(2)
  tq = MOBA_BLOCK
  q2 = q_ref[0]
  lane = lax.broadcasted_iota(jnp.int32, (tq, LANES), 1)
  qscale = (ATTN_HEAD_DIM ** -0.5) * LOG2E

  lhs = []
  for h in range(2):
    in_head = (lane >= ATTN_HEAD_DIM * h) & (lane < ATTN_HEAD_DIM * (h + 1))
    qh = jnp.where(in_head, q2, 0.0)
    gate = jnp.dot(qh, kmt_ref[0], preferred_element_type=F32, precision=HIGHEST)
    g = jnp.where(lane < j, gate, -jnp.inf)
    sel = lane == j
    for _ in range(MOBA_TOPK):
      mx = jnp.max(g, axis=1, keepdims=True)
      idx = jnp.min(jnp.where(g == mx, lane, LANES), axis=1, keepdims=True)
      pick = (lane == idx) & (g > -jnp.inf)
      sel = sel | pick
      g = jnp.where(pick, -jnp.inf, g)
    side = jnp.where(sel, 0.0, NEG_BIG)
    ones_at = (lane >= nb + 3 * h) & (lane < nb + 3 * h + 3)
    side = jnp.where(lane < nb, side, jnp.where(ones_at, 1.0, 0.0))
    lhs.append(jnp.concatenate([(qh * qscale).astype(BF16), side.astype(BF16)], axis=1))

  def scores(h, off):
    rhs = jnp.concatenate([k_ref[0, pl.ds(off, tq), :], kb_ref[0, pl.ds(off, tq), :]], axis=1)
    return lax.dot_general(lhs[h], rhs, (((1,), (1,)), ((), ())), preferred_element_type=F32)

  own = pl.multiple_of(j * tq, tq)
  r = lax.broadcasted_iota(jnp.int32, (tq, tq), 0)
  c = lax.broadcasted_iota(jnp.int32, (tq, tq), 1)
  v_own = v_ref[0, pl.ds(own, tq), :]
  for h in range(2):
    s = jnp.where(c <= r, scores(h, own), -jnp.inf)
    m = jnp.max(s, axis=1, keepdims=True)
    p = jnp.exp2(s - m)
    m_scr[h] = m
    l_scr[h] = jnp.sum(p, axis=1, keepdims=True)
    acc_scr[h] = jnp.dot(p.astype(BF16), v_own, preferred_element_type=F32)

  def past(n, carry):
    off = pl.multiple_of(n * tq, tq)
    vv = v_ref[0, pl.ds(off, tq), :]
    for h in range(2):
      s = scores(h, off)
      m_old = m_scr[h]
      m_new = jnp.maximum(m_old, jnp.max(s, axis=1, keepdims=True))
      alpha = jnp.exp2(m_old - m_new)
      p = jnp.exp2(s - m_new)
      m_scr[h] = m_new
      l_scr[h] = alpha * l_scr[h] + jnp.sum(p, axis=1, keepdims=True)
      acc_scr[h] = alpha * acc_scr[h] + jnp.dot(p.astype(BF16), vv, preferred_element_type=F32)
    return carry

  lax.fori_loop(0, j, past, 0)
  o0 = acc_scr[0] / l_scr[0]
  o1 = acc_scr[1] / l_scr[1]
  o_ref[0] = jnp.where(lane < ATTN_HEAD_DIM, o0, o1).astype(o_ref.dtype)


def _moba_attention(q, k, v, kmean):
  bsz, s, aw = q.shape
  nb = s // MOBA_BLOCK
  assert s % MOBA_BLOCK == 0 and 2 * ATTN_HEAD_DIM == LANES
  kb = _key_side_table(s)
  kmt = jnp.pad(jnp.swapaxes(kmean, 1, 2), ((0, 0), (0, 0), (0, LANES - nb)))
  tq = MOBA_BLOCK
  return pl.pallas_call(
      functools.partial(_attn_kernel, nb=nb),
      grid=(bsz, N_ATTN_HEADS // 2, nb),
      in_specs=[
          pl.BlockSpec((1, tq, LANES), lambda b, hp, j: (b, j, hp)),
          pl.BlockSpec((1, s, LANES), lambda b, hp, j: (b, 0, hp)),
          pl.BlockSpec((1, s, LANES), lambda b, hp, j: (b, 0, hp)),
          pl.BlockSpec((1, s, LANES), lambda b, hp, j: (hp, 0, 0)),
          pl.BlockSpec((1, LANES, LANES), lambda b, hp, j: (b, hp, 0)),
      ],
      out_specs=pl.BlockSpec((1, tq, LANES), lambda b, hp, j: (b, j, hp)),
      out_shape=jax.ShapeDtypeStruct((bsz, s, aw), BF16),
      scratch_shapes=[pltpu.VMEM((2, tq, 1), F32), pltpu.VMEM((2, tq, 1), F32),
                      pltpu.VMEM((2, tq, LANES), F32)],
      compiler_params=_params("arbitrary", "arbitrary", "arbitrary"),
      name="moba_attn",
  )(q, k, v, kb, kmt)


def _split3(x):
  hi = x.astype(BF16)
  r = x - hi.astype(F32)
  mid = r.astype(BF16)
  lo = (r - mid.astype(F32)).astype(BF16)
  return hi, mid, lo


def _sgu_kernel(gu_ref, gv_ref, avg_ref, lng_ref, lnb_ref, wsp_ref, bsp_ref, o_ref):
  tm, width = gu_ref.shape
  avg = avg_ref[...]

  def group_mean(a):
    return sum(jnp.dot(p, avg, preferred_element_type=F32) for p in _split3(a))

  v = _gelu(gv_ref[...])
  d = v - group_mean(v)
  vv = (d * lax.rsqrt(group_mean(d * d) + LN_EPS) * lng_ref[...] + lnb_ref[...]).astype(BF16)
  u = _gelu(gu_ref[...])

  rows = lax.broadcasted_iota(jnp.int32, (GMLP_CHUNK, GMLP_CHUNK), 0)
  cols = lax.broadcasted_iota(jnp.int32, (GMLP_CHUNK, GMLP_CHUNK), 1)
  grp = lax.broadcasted_iota(jnp.int32, (GMLP_CHUNK, width), 1) // GMLP_GROUP_DIM
  ws = [jnp.where(rows >= cols, wsp_ref[g], 0.0).astype(BF16) for g in range(GMLP_GROUPS)]
  for ck in range(tm // GMLP_CHUNK):
    sl = slice(ck * GMLP_CHUNK, (ck + 1) * GMLP_CHUNK)
    vc = vv[sl]
    sv = bsp_ref[...]
    for g in range(GMLP_GROUPS):
      sv = sv + jnp.where(grp == g, jnp.dot(ws[g], vc, preferred_element_type=F32), 0.0)
    o_ref[sl, :] = (u[sl] * sv).astype(o_ref.dtype)


def _sgu(gu, gv, ln_g, ln_b, w_spatial, b_spatial):
  t, width = gu.shape
  tm = 512
  assert t % tm == 0 and tm % GMLP_CHUNK == 0 and width == GMLP_GROUPS * GMLP_GROUP_DIM
  ch = jnp.arange(width) // GMLP_GROUP_DIM
  avg = ((ch[:, None] == ch[None, :]).astype(F32) / GMLP_GROUP_DIM).astype(BF16)
  bsp = jnp.repeat(b_spatial.T, GMLP_GROUP_DIM, axis=1)
  row = pl.BlockSpec((tm, width), lambda i: (i, 0))
  full = lambda a: pl.BlockSpec(a.shape, lambda i: (0,) * a.ndim)
  args = (gu, gv, avg, ln_g.reshape(1, width), ln_b.reshape(1, width), w_spatial, bsp)
  return pl.pallas_call(
      _sgu_kernel,
      grid=(t // tm,),
      in_specs=[row, row] + [full(a) for a in args[2:]],
      out_specs=row,
      out_shape=jax.ShapeDtypeStruct((t, width), BF16),
      compiler_params=_params("arbitrary"),
      name="sgu",
  )(*args)


def _merge_kernel(attn_ref, sgu_ref, ga_ref, gg_ref, x_ref, gt1_ref, sh2_ref, sc2_ref, g1_ref, b1_ref,
                  wa_ref, wg_ref, wo_ref, wq_ref, keys_ref, x1_ref, h2_ref, st_ref, *, alpha):
  a = jnp.dot(attn_ref[0], wa_ref[...], preferred_element_type=F32)
  g = jnp.dot(sgu_ref[0], wg_ref[...], preferred_element_type=F32)
  merged = jax.nn.sigmoid(ga_ref[0]) * a + jax.nn.sigmoid(gg_ref[0]) * g
  out = jnp.dot(merged.astype(BF16), wo_ref[...], preferred_element_type=F32)
  x1 = _layernorm(alpha * x_ref[0] + (1.0 + gt1_ref[0]) * out) * g1_ref[...] + b1_ref[...]
  x1_ref[0] = x1
  h2 = (_layernorm(x1) * (1.0 + sc2_ref[0]) + sh2_ref[0]).astype(BF16)
  h2_ref[0] = h2
  qp = jnp.dot(h2, wq_ref[...], preferred_element_type=F32).astype(BF16)
  half = keys_ref.shape[2]
  for hc in range(keys_ref.shape[0]):
    st_ref[hc] = lax.dot_general(keys_ref[hc], qp[:, hc * half:(hc + 1) * half],
                                 (((1,), (1,)), ((), ())), preferred_element_type=F32)


def _merge_out(attn, sgu, ga, gg, x, gt1, sh2, sc2, ln1_g, ln1_b, w_attn_up, w_gmlp_up, w_out, w_peer_q,
               sub_keys, alpha):
  bsz, s, d = x.shape
  tm = 256
  assert s % tm == 0
  nt = s // tm
  hc, nk, half = sub_keys.shape[0] * sub_keys.shape[1], sub_keys.shape[2], sub_keys.shape[3]
  keys = sub_keys.reshape(hc, nk, half).astype(BF16)
  ws = [w.astype(BF16) for w in (w_attn_up, w_gmlp_up, w_out, w_peer_q)]
  assert w_peer_q.shape[1] == hc * half
  row = lambda width: pl.BlockSpec((1, tm, width), lambda b, i: (b, i, 0))
  vec = pl.BlockSpec((1, 1, d), lambda b, i: (b, 0, 0))
  full = lambda a: pl.BlockSpec(a.shape, lambda b, i: (0,) * a.ndim)
  aw = attn.shape[-1]
  return pl.pallas_call(
      functools.partial(_merge_kernel, alpha=alpha),
      grid=(bsz, nt),
      in_specs=[row(aw), row(sgu.shape[-1]), row(d), row(d), row(d), vec, vec, vec,
                full(ln1_g), full(ln1_b)] + [full(w) for w in ws] + [full(keys)],
      out_specs=[row(d), row(d), pl.BlockSpec((hc, nk, tm), lambda b, i: (0, 0, b * nt + i))],
      out_shape=[jax.ShapeDtypeStruct((bsz, s, d), F32), jax.ShapeDtypeStruct((bsz, s, d), BF16),
                 jax.ShapeDtypeStruct((hc, nk, bsz * s), F32)],
      compiler_params=_params("arbitrary", "arbitrary"),
      name="merge_out",
  )(attn, sgu, ga, gg, x, gt1, sh2, sc2, ln1_g, ln1_b, *ws, keys)


def _young_cells():
  return [(a, b) for a in range(PEER_TOPK) for b in range(PEER_TOPK) if (a + 1) * (b + 1) <= PEER_TOPK]


def _route_kernel(st_ref, ca_ref, na_ref, eb_ref, rb_ref, tv_scr, cell_scr):
  nk, tn = st_ref.shape[1], st_ref.shape[2]
  row = lax.broadcasted_iota(jnp.int32, (nk, tn), 0)
  cells = _young_cells()
  npad = cell_scr.shape[0]
  crow = lax.broadcasted_iota(jnp.int32, (npad, tn), 0)
  kf = float(PEER_TOPK)

  def top_sorted(s, half):
    rank = jnp.full((nk, tn), float(nk), F32)
    for r in range(PEER_TOPK):
      mx = jnp.max(s, axis=0, keepdims=True)
      idx = jnp.min(jnp.where(s == mx, row, nk), axis=0, keepdims=True)
      pick = row == idx
      rank = jnp.where(pick, float(r), rank)
      s = jnp.where(pick, -jnp.inf, s)
      tv_scr[half, r:r + 1, :] = mx
    return rank

  def head(h, carry):
    rank0 = top_sorted(st_ref[2 * h], 0)
    rank1 = top_sorted(st_ref[2 * h + 1], 1)
    tv0 = tv_scr[0]
    tv1 = tv_scr[1]
    for ci, (a, b) in enumerate(cells):
      cell_scr[ci:ci + 1, :] = tv0[a:a + 1] + tv1[b:b + 1]
    if npad > len(cells):
      cell_scr[len(cells):, :] = jnp.full((npad - len(cells), tn), -jnp.inf, F32)
    cand = cell_scr[...]
    top = tv0[0:1] + tv1[0:1]
    picked = jnp.zeros((npad, tn), F32)
    z = jnp.zeros((1, tn), F32)
    for _ in range(PEER_TOPK):
      mx = jnp.max(cand, axis=0, keepdims=True)
      idx = jnp.min(jnp.where(cand == mx, crow, npad), axis=0, keepdims=True)
      pick = crow == idx
      picked = jnp.where(pick, 1.0, picked)
      cand = jnp.where(pick, -jnp.inf, cand)
      z = z + jnp.exp(mx - top)
    cell_scr[...] = picked
    na = jnp.zeros((nk, tn), F32)
    start = 0
    for a in range(PEER_TOPK):
      cnt = sum(1 for (aa, _) in cells if aa == a)
      n_a = jnp.sum(cell_scr[start:start + cnt, :], axis=0, keepdims=True)
      na = jnp.where(rank0 == float(a), n_a, na)
      start += cnt
    ca_ref[h] = jnp.where(rank0 < kf, jnp.exp(st_ref[2 * h] - tv0[0:1]) / z, 0.0)
    na_ref[h] = na
    eb_ref[h] = jnp.exp(st_ref[2 * h + 1] - tv1[0:1])
    rb_ref[h] = rank1
    return carry

  lax.fori_loop(0, ca_ref.shape[0], head, 0)


def _peer_route(st):
  hc, nk, t = st.shape
  heads = hc // 2
  tn = LANES
  assert t % tn == 0
  npad = -(-len(_young_cells()) // 8) * 8
  out = jax.ShapeDtypeStruct((heads, nk, t), F32)
  ospec = pl.BlockSpec((heads, nk, tn), lambda i: (0, 0, i))
  return pl.pallas_call(
      _route_kernel,
      grid=(t // tn,),
      in_specs=[pl.BlockSpec((hc, nk, tn), lambda i: (0, 0, i))],
      out_specs=[ospec] * 4,
      out_shape=[out] * 4,
      scratch_shapes=[pltpu.VMEM((2, PEER_TOPK, tn), F32), pltpu.VMEM((npad, tn), F32)],
      compiler_params=_params("arbitrary"),
      name="peer_route",
  )(st)


def _peer_dense_kernel(h2_ref, u_ref, v_ref, ca_ref, na_ref, eb_ref, rb_ref, x1_ref, gt2_ref, g2_ref, b2_ref,
                       o_ref, acc_scr, at_scr, p_scr, *, alpha):
  e = pl.program_id(1)
  ec, tn = at_scr.shape
  nk = eb_ref.shape[1]
  heads = eb_ref.shape[0]
  per_step = ec // nk

  @pl.when(e == 0)
  def _():
    acc_scr[...] = jnp.zeros_like(acc_scr)

  at_scr[...] = lax.dot_general(u_ref[...], h2_ref[...], (((1,), (1,)), ((), ())), preferred_element_type=F32)
  for ii in range(per_step):
    rows = slice(ii * nk, (ii + 1) * nk)
    for lc in range(tn // LANES):
      ls = slice(lc * LANES, (lc + 1) * LANES)
      w = jnp.zeros((nk, LANES), F32)
      for h in range(heads):
        ca = ca_ref[h, ii:ii + 1, ls]
        na = na_ref[h, ii:ii + 1, ls]
        w = w + ca * jnp.where(rb_ref[h, :, ls] < na, eb_ref[h, :, ls], 0.0)
      p_scr[rows, ls] = (w * _gelu(at_scr[rows, ls])).astype(BF16)
  acc_scr[...] += lax.dot_general(p_scr[...], v_ref[...], (((0,), (0,)), ((), ())), preferred_element_type=F32)

  @pl.when(e == pl.num_programs(1) - 1)
  def _():
    y = alpha * x1_ref[...] + (1.0 + gt2_ref[0]) * acc_scr[...]
    o_ref[...] = _layernorm(y) * g2_ref[...] + b2_ref[...]


def _peer_dense(h2, u, v, ca, na, eb, rb, x1, gt2, ln2_g, ln2_b, seq, alpha):
  t, d = h2.shape
  n_exp = u.shape[0]
  heads, nk, _ = ca.shape
  assert n_exp == nk * nk
  per_step = 8
  tn, ec = 512, per_step * nk
  assert seq % tn == 0 and n_exp % ec == 0
  ub, vb = u.astype(BF16), v.astype(BF16)
  tok = lambda width: pl.BlockSpec((tn, width), lambda i, e: (i, 0))
  expert = pl.BlockSpec((ec, d), lambda i, e: (e, 0))
  fac = pl.BlockSpec((heads, nk, tn), lambda i, e: (0, 0, i))
  fac_i = pl.BlockSpec((heads, per_step, tn), lambda i, e: (0, e, i))
  one = pl.BlockSpec((1, d), lambda i, e: (0, 0))
  return pl.pallas_call(
      functools.partial(_peer_dense_kernel, alpha=alpha),
      grid=(t // tn, n_exp // ec),
      in_specs=[tok(d), expert, expert, fac_i, fac_i, fac, fac, tok(d),
                pl.BlockSpec((1, 1, d), lambda i, e: ((i * tn) // seq, 0, 0)), one, one],
      out_specs=tok(d),
      out_shape=jax.ShapeDtypeStruct((t, d), F32),
      scratch_shapes=[pltpu.VMEM((tn, d), F32), pltpu.VMEM((ec, tn), F32), pltpu.VMEM((ec, tn), BF16)],
      compiler_params=_params("arbitrary", "arbitrary"),
      name="peer_dense",
  )(h2, ub, vb, ca, na, eb, rb, x1, gt2, ln2_g, ln2_b)


def kernel(x, c, w_cond, b_cond, w_in, w_attn_up, w_gmlp_up, w_out, sgu_ln_g, sgu_ln_b, w_spatial, b_spatial,
           ln1_g, ln1_b, w_peer_q, peer_sub_keys, peer_u, peer_v, ln2_g, ln2_b):
  bsz, s, d = x.shape
  t = bsz * s
  depth = w_cond.shape[0]
  alpha = float((2.0 * depth) ** 0.25)
  for l in range(depth):
    mod = _cond(c, w_cond[l], b_cond[l])
    sh1, sc1, gt1, sh2, sc2, gt2 = [m[:, None, :] for m in jnp.split(mod, 6, axis=-1)]
    q, k, v, kmean, gu, gv, ga, gg = _in_proj(x, sh1, sc1, w_in[l])
    attn = _moba_attention(q, k, v, kmean)
    sgu = _sgu(gu.reshape(t, -1), gv.reshape(t, -1), sgu_ln_g[l], sgu_ln_b[l], w_spatial[l], b_spatial[l])
    x1, h2, st = _merge_out(attn, sgu.reshape(bsz, s, -1), ga, gg, x, gt1, sh2, sc2,
                            ln1_g[l].reshape(1, d), ln1_b[l].reshape(1, d),
                            w_attn_up[l], w_gmlp_up[l], w_out[l], w_peer_q[l], peer_sub_keys[l], alpha)
    ca, na, eb, rb = _peer_route(st)
    x = _peer_dense(h2.reshape(t, d), peer_u[l], peer_v[l], ca, na, eb, rb, x1.reshape(t, d), gt2,
                    ln2_g[l].reshape(1, d), ln2_b[l].reshape(1, d), s, alpha).reshape(bsz, s, d)
  return x
```

```python
import functools

import jax
import jax.numpy as jnp
from jax import lax
from jax.experimental import pallas as pl
from jax.experimental.pallas import tpu as pltpu

F32 = jnp.float32
BF16 = jnp.bfloat16
HIGHEST = lax.Precision.HIGHEST

N_ATTN_HEADS = 8
ATTN_HEAD_DIM = 64
MOBA_BLOCK = 256
MOBA_TOPK = 3
GMLP_GROUPS = 8
GMLP_GROUP_DIM = 64
GMLP_CHUNK = 128
PEER_HEADS = 8
PEER_N_KEYS = 128
PEER_TOPK = 16
LN_EPS = 1e-5

LANES = 128
V7X_VMEM_LIMIT_BYTES = 56 * 1024 * 1024

NEG_BIG = -1e30
LOG2E = 1.4426950408889634


def _params(*sem):
  return pltpu.CompilerParams(dimension_semantics=sem, vmem_limit_bytes=V7X_VMEM_LIMIT_BYTES)


def _layernorm(x):
  mu = jnp.mean(x, axis=-1, keepdims=True)
  d = x - mu
  var = jnp.mean(d * d, axis=-1, keepdims=True)
  return d * lax.rsqrt(var + LN_EPS)


def _gelu(x):
  return 0.5 * x * (1.0 + lax.erf(x * (2.0 ** -0.5)))


def _cond_kernel(c_ref, w_ref, b_ref, o_ref):
  cs = jax.nn.silu(c_ref[...])
  o_ref[...] = jnp.dot(cs, w_ref[...], preferred_element_type=F32, precision=HIGHEST) + b_ref[...]


def _cond(c, w, b):
  bsz, d = c.shape
  n = w.shape[1]
  rows = 8
  cp = jnp.zeros((rows, d), F32).at[:bsz].set(c)
  tn = 768 if n % 768 == 0 else n
  out = pl.pallas_call(
      _cond_kernel,
      grid=(n // tn,),
      in_specs=[
          pl.BlockSpec((rows, d), lambda j: (0, 0)),
          pl.BlockSpec((d, tn), lambda j: (0, j)),
          pl.BlockSpec((1, tn), lambda j: (0, j)),
      ],
      out_specs=pl.BlockSpec((rows, tn), lambda j: (0, j)),
      out_shape=jax.ShapeDtypeStruct((rows, n), F32),
      compiler_params=_params("arbitrary"),
      name="adaln_cond",
  )(cp, w, b.reshape(1, n))
  return out[:bsz]


def _inproj_kernel(x_ref, sh_ref, sc_ref, wq, wk, wv, wgu, wgv, wga, wgg,
                   q_ref, k_ref, v_ref, km_ref, gu_ref, gv_ref, ga_ref, gg_ref):
  h = (_layernorm(x_ref[0]) * (1.0 + sc_ref[0]) + sh_ref[0]).astype(BF16)
  dot = lambda w: jnp.dot(h, w[...], preferred_element_type=F32)
  q_ref[0] = dot(wq).T
  k = dot(wk)
  tm, width = k.shape
  km_ref[0, 0] = jnp.mean(k.reshape(tm // MOBA_BLOCK, MOBA_BLOCK, width), axis=1)
  k_ref[0] = k.astype(BF16)
  v = dot(wv)
  for kb in range(tm // MOBA_BLOCK):
    v_ref[0, kb] = v[kb * MOBA_BLOCK:(kb + 1) * MOBA_BLOCK].T.astype(BF16)
  gu_ref[0] = dot(wgu)
  gv_ref[0] = dot(wgv)
  ga_ref[0] = dot(wga)
  gg_ref[0] = dot(wgg)


def _in_proj(x, sh, sc, w_in):
  bsz, s, d = x.shape
  aw = N_ATTN_HEADS * ATTN_HEAD_DIM
  gw = GMLP_GROUPS * GMLP_GROUP_DIM
  bounds = [0, aw, 2 * aw, 3 * aw, 3 * aw + gw, 3 * aw + 2 * gw, 3 * aw + 2 * gw + d, 3 * aw + 2 * gw + 2 * d]
  assert w_in.shape[1] == bounds[-1]
  ws = [w_in[:, a:b].astype(BF16) for a, b in zip(bounds[:-1], bounds[1:])]
  tm = 512
  assert s % tm == 0 and tm % MOBA_BLOCK == 0
  nt = s // tm
  row = lambda width: pl.BlockSpec((1, tm, width), lambda b, i: (b, i, 0))
  vec = pl.BlockSpec((1, 1, d), lambda b, i: (b, 0, 0))
  wspec = lambda w: pl.BlockSpec(w.shape, lambda b, i: (0, 0))
  sds = lambda width, dt: jax.ShapeDtypeStruct((bsz, s, width), dt)
  kpt = tm // MOBA_BLOCK
  outs = pl.pallas_call(
      _inproj_kernel,
      grid=(bsz, nt),
      in_specs=[row(d), vec, vec] + [wspec(w) for w in ws],
      out_specs=[pl.BlockSpec((1, aw, tm), lambda b, i: (b, 0, i)), row(aw),
                 pl.BlockSpec((1, kpt, aw, MOBA_BLOCK), lambda b, i: (b, i, 0, 0)),
                 pl.BlockSpec((1, 1, kpt, aw), lambda b, i: (b, i, 0, 0)),
                 row(gw), row(gw), row(d), row(d)],
      out_shape=[jax.ShapeDtypeStruct((bsz, aw, s), F32), sds(aw, BF16),
                 jax.ShapeDtypeStruct((bsz, s // MOBA_BLOCK, aw, MOBA_BLOCK), BF16),
                 jax.ShapeDtypeStruct((bsz, nt, kpt, aw), F32),
                 sds(gw, F32), sds(gw, F32), sds(d, F32), sds(d, F32)],
      compiler_params=_params("arbitrary", "arbitrary"),
      name="in_proj",
  )(x, sh, sc, *ws)
  q, k, v, km, gu, gv, ga, gg = outs
  return q, k, v, km.reshape(bsz, s // MOBA_BLOCK, aw), gu, gv, ga, gg


def _alibi_slopes():
  return [2.0 ** (-8.0 * (i + 1) / N_ATTN_HEADS) for i in range(N_ATTN_HEADS)]


def _key_side_table(s):
  nb = s // MOBA_BLOCK
  assert nb + 6 <= LANES
  pos = jnp.arange(s, dtype=F32)
  blk = jnp.arange(s, dtype=jnp.int32) // MOBA_BLOCK
  onehot = (blk[:, None] == jnp.arange(nb)[None, :]).astype(F32)
  slopes = _alibi_slopes()

  def top16(a):
    bits = lax.bitcast_convert_type(a, jnp.uint32) & jnp.uint32(0xFFFF0000)
    return lax.bitcast_convert_type(bits, F32)

  tables = []
  for hp in range(N_ATTN_HEADS // 2):
    cols = [onehot]
    for h in (2 * hp, 2 * hp + 1):
      c = (slopes[h] * LOG2E) * pos
      hi = top16(c)
      mid = top16(c - hi)
      lo = c - hi - mid
      cols += [hi[:, None], mid[:, None], lo[:, None]]
    t = jnp.concatenate(cols, axis=1)
    tables.append(jnp.pad(t, ((0, 0), (0, LANES - t.shape[1]))))
  return jnp.stack(tables).astype(BF16)


ATTN_BLOCKS_PER_STEP = 4


def _attn_kernel(qt_ref, k_ref, kb_ref, vt_ref, km_ref, o_ref, m_scr, l_scr, acc_scr, sa_scr, sb_scr, *, nb):
  j = pl.program_id(2)
  tq = MOBA_BLOCK
  hd = ATTN_HEAD_DIM
  q2 = qt_ref[0]
  row = lax.broadcasted_iota(jnp.int32, (LANES, tq), 0)
  qscale = (hd ** -0.5) * LOG2E

  rhs_own, rhs_past = [], []
  for h in range(2):
    qh = jnp.where((row >= hd * h) & (row < hd * (h + 1)), q2, 0.0)
    gate = jnp.dot(km_ref[0], qh, preferred_element_type=F32, precision=HIGHEST)
    g = jnp.where(row < j, gate, -jnp.inf)
    sel = row < 0
    for _ in range(MOBA_TOPK):
      mx = jnp.max(g, axis=0, keepdims=True)
      idx = jnp.min(jnp.where(g == mx, row, LANES), axis=0, keepdims=True)
      pick = (row == idx) & (g > -jnp.inf)
      sel = sel | pick
      g = jnp.where(pick, -jnp.inf, g)
    ones_at = (row >= nb + 3 * h) & (row < nb + 3 * h + 3)
    tail = jnp.where(ones_at, 1.0, 0.0)
    qb = (qh * qscale).astype(BF16)
    side_past = jnp.where(row < nb, jnp.where(sel, 0.0, NEG_BIG), tail)
    side_own = jnp.where(row < nb, jnp.where(row == j, 0.0, NEG_BIG), tail)
    rhs_past.append(jnp.concatenate([qb, side_past.astype(BF16)], axis=0))
    rhs_own.append(jnp.concatenate([qb, side_own.astype(BF16)], axis=0))

  def keys(off, n):
    return jnp.concatenate([k_ref[0, pl.ds(off, n), :], kb_ref[0, pl.ds(off, n), :]], axis=1)

  own = pl.multiple_of(j * tq, tq)
  kr = lax.broadcasted_iota(jnp.int32, (tq, tq), 0)
  qc = lax.broadcasted_iota(jnp.int32, (tq, tq), 1)
  k_own = keys(own, tq)
  for h in range(2):
    s = jnp.where(kr <= qc, jnp.dot(k_own, rhs_own[h], preferred_element_type=F32), -jnp.inf)
    m = jnp.max(s, axis=0, keepdims=True)
    p = jnp.exp2(s - m)
    m_scr[h] = m
    l_scr[h] = jnp.sum(p, axis=0, keepdims=True)
    acc_scr[h] = jnp.dot(vt_ref[0, j, hd * h:hd * (h + 1), :], p.astype(BF16), preferred_element_type=F32)

  per = ATTN_BLOCKS_PER_STEP
  n_chunks = (j + per - 1) // per
  c_last = nb // per - 1

  def qk(c, s_ref):
    kk = keys(pl.multiple_of(c * (per * tq), per * tq), per * tq)
    for h in range(2):
      s_ref[h] = jnp.dot(kk, rhs_past[h], preferred_element_type=F32)

  def softmax_pv(c, s_ref):
    for h in range(2):
      s = s_ref[h]
      m_old = m_scr[h]
      m_new = jnp.maximum(m_old, jnp.max(s, axis=0, keepdims=True))
      alpha = jnp.exp2(m_old - m_new)
      p = jnp.exp2(s - m_new)
      m_scr[h] = m_new
      l_scr[h] = alpha * l_scr[h] + jnp.sum(p, axis=0, keepdims=True)
      p = p.astype(BF16)
      pv = sum(jnp.dot(vt_ref[0, c * per + i, hd * h:hd * (h + 1), :], p[i * tq:(i + 1) * tq],
                       preferred_element_type=F32) for i in range(per))
      acc_scr[h] = alpha * acc_scr[h] + pv

  @pl.when(n_chunks > 0)
  def _():
    qk(0, sa_scr)

  def pair(i, carry):
    qk(2 * i + 1, sb_scr)
    softmax_pv(2 * i, sa_scr)
    qk(jnp.minimum(2 * i + 2, c_last), sa_scr)
    softmax_pv(2 * i + 1, sb_scr)
    return carry

  lax.fori_loop(0, (n_chunks + 1) // 2, pair, 0)
  out_t = jnp.concatenate([acc_scr[0] / l_scr[0], acc_scr[1] / l_scr[1]], axis=0)
  o_ref[0] = out_t.T.astype(o_ref.dtype)


def _moba_attention(qt, k, vt, kmean):
  bsz, aw, s = qt.shape
  nb = s // MOBA_BLOCK
  per = ATTN_BLOCKS_PER_STEP
  assert s % MOBA_BLOCK == 0 and 2 * ATTN_HEAD_DIM == LANES and nb % (2 * per) == 0
  kb = _key_side_table(s)
  km = jnp.pad(kmean, ((0, 0), (0, LANES - nb), (0, 0)))
  tq = MOBA_BLOCK
  return pl.pallas_call(
      functools.partial(_attn_kernel, nb=nb),
      grid=(bsz, N_ATTN_HEADS // 2, nb),
      in_specs=[
          pl.BlockSpec((1, LANES, tq), lambda b, hp, j: (b, hp, j)),
          pl.BlockSpec((1, s, LANES), lambda b, hp, j: (b, 0, hp)),
          pl.BlockSpec((1, s, LANES), lambda b, hp, j: (hp, 0, 0)),
          pl.BlockSpec((1, nb, LANES, tq), lambda b, hp, j: (b, 0, hp, 0)),
          pl.BlockSpec((1, LANES, LANES), lambda b, hp, j: (b, 0, hp)),
      ],
      out_specs=pl.BlockSpec((1, tq, LANES), lambda b, hp, j: (b, j, hp)),
      out_shape=jax.ShapeDtypeStruct((bsz, s, aw), BF16),
      scratch_shapes=[pltpu.VMEM((2, 1, tq), F32), pltpu.VMEM((2, 1, tq), F32),
                      pltpu.VMEM((2, ATTN_HEAD_DIM, tq), F32),
                      pltpu.VMEM((2, per * tq, tq), F32), pltpu.VMEM((2, per * tq, tq), F32)],
      compiler_params=_params("arbitrary", "arbitrary", "arbitrary"),
      name="moba_attn",
  )(qt, k, kb, vt, km)


def _split3(x):
  hi = x.astype(BF16)
  r = x - hi.astype(F32)
  mid = r.astype(BF16)
  lo = (r - mid.astype(F32)).astype(BF16)
  return hi, mid, lo


def _sgu_kernel(gu_ref, gv_ref, avg_ref, lng_ref, lnb_ref, wsp_ref, bsp_ref, o_ref):
  tm, width = gu_ref.shape
  avg = avg_ref[...]

  def group_mean(a):
    return sum(jnp.dot(p, avg, preferred_element_type=F32) for p in _split3(a))

  v = _gelu(gv_ref[...])
  d = v - group_mean(v)
  vv = (d * lax.rsqrt(group_mean(d * d) + LN_EPS) * lng_ref[...] + lnb_ref[...]).astype(BF16)
  u = _gelu(gu_ref[...])

  rows = lax.broadcasted_iota(jnp.int32, (GMLP_CHUNK, GMLP_CHUNK), 0)
  cols = lax.broadcasted_iota(jnp.int32, (GMLP_CHUNK, GMLP_CHUNK), 1)
  grp = lax.broadcasted_iota(jnp.int32, (GMLP_CHUNK, width), 1) // GMLP_GROUP_DIM
  ws = [jnp.where(rows >= cols, wsp_ref[g], 0.0).astype(BF16) for g in range(GMLP_GROUPS)]
  for ck in range(tm // GMLP_CHUNK):
    sl = slice(ck * GMLP_CHUNK, (ck + 1) * GMLP_CHUNK)
    vc = vv[sl]
    sv = bsp_ref[...]
    for g in range(GMLP_GROUPS):
      sv = sv + jnp.where(grp == g, jnp.dot(ws[g], vc, preferred_element_type=F32), 0.0)
    o_ref[sl, :] = (u[sl] * sv).astype(o_ref.dtype)


def _sgu(gu, gv, ln_g, ln_b, w_spatial, b_spatial):
  t, width = gu.shape
  tm = 512
  assert t % tm == 0 and tm % GMLP_CHUNK == 0 and width == GMLP_GROUPS * GMLP_GROUP_DIM
  ch = jnp.arange(width) // GMLP_GROUP_DIM
  avg = ((ch[:, None] == ch[None, :]).astype(F32) / GMLP_GROUP_DIM).astype(BF16)
  bsp = jnp.repeat(b_spatial.T, GMLP_GROUP_DIM, axis=1)
  row = pl.BlockSpec((tm, width), lambda i: (i, 0))
  full = lambda a: pl.BlockSpec(a.shape, lambda i: (0,) * a.ndim)
  args = (gu, gv, avg, ln_g.reshape(1, width), ln_b.reshape(1, width), w_spatial, bsp)
  return pl.pallas_call(
      _sgu_kernel,
      grid=(t // tm,),
      in_specs=[row, row] + [full(a) for a in args[2:]],
      out_specs=row,
      out_shape=jax.ShapeDtypeStruct((t, width), BF16),
      compiler_params=_params("arbitrary"),
      name="sgu",
  )(*args)


def _merge_kernel(attn_ref, sgu_ref, ga_ref, gg_ref, x_ref, gt1_ref, sh2_ref, sc2_ref, g1_ref, b1_ref,
                  wa_ref, wg_ref, wo_ref, wq_ref, keys_ref, x1_ref, h2t_ref, st_ref, *, alpha):
  a = jnp.dot(attn_ref[0], wa_ref[...], preferred_element_type=F32)
  g = jnp.dot(sgu_ref[0], wg_ref[...], preferred_element_type=F32)
  merged = jax.nn.sigmoid(ga_ref[0]) * a + jax.nn.sigmoid(gg_ref[0]) * g
  out = jnp.dot(merged.astype(BF16), wo_ref[...], preferred_element_type=F32)
  x1 = _layernorm(alpha * x_ref[0] + (1.0 + gt1_ref[0]) * out) * g1_ref[...] + b1_ref[...]
  x1_ref[0] = x1
  h2f = _layernorm(x1) * (1.0 + sc2_ref[0]) + sh2_ref[0]
  h2t_ref[...] = h2f.T.astype(BF16)
  h2 = h2f.astype(BF16)
  qp = jnp.dot(h2, wq_ref[...], preferred_element_type=F32).astype(BF16)
  half = keys_ref.shape[2]
  for hc in range(keys_ref.shape[0]):
    st_ref[hc] = lax.dot_general(keys_ref[hc], qp[:, hc * half:(hc + 1) * half],
                                 (((1,), (1,)), ((), ())), preferred_element_type=F32)


def _merge_out(attn, sgu, ga, gg, x, gt1, sh2, sc2, ln1_g, ln1_b, w_attn_up, w_gmlp_up, w_out, w_peer_q,
               sub_keys, alpha):
  bsz, s, d = x.shape
  tm = 256
  assert s % tm == 0
  nt = s // tm
  hc, nk, half = sub_keys.shape[0] * sub_keys.shape[1], sub_keys.shape[2], sub_keys.shape[3]
  keys = sub_keys.reshape(hc, nk, half).astype(BF16)
  ws = [w.astype(BF16) for w in (w_attn_up, w_gmlp_up, w_out, w_peer_q)]
  assert w_peer_q.shape[1] == hc * half
  row = lambda width: pl.BlockSpec((1, tm, width), lambda b, i: (b, i, 0))
  vec = pl.BlockSpec((1, 1, d), lambda b, i: (b, 0, 0))
  full = lambda a: pl.BlockSpec(a.shape, lambda b, i: (0,) * a.ndim)
  aw = attn.shape[-1]
  return pl.pallas_call(
      functools.partial(_merge_kernel, alpha=alpha),
      grid=(bsz, nt),
      in_specs=[row(aw), row(sgu.shape[-1]), row(d), row(d), row(d), vec, vec, vec,
                full(ln1_g), full(ln1_b)] + [full(w) for w in ws] + [full(keys)],
      out_specs=[row(d), pl.BlockSpec((d, tm), lambda b, i: (0, b * nt + i)),
                 pl.BlockSpec((hc, nk, tm), lambda b, i: (0, 0, b * nt + i))],
      out_shape=[jax.ShapeDtypeStruct((bsz, s, d), F32), jax.ShapeDtypeStruct((d, bsz * s), BF16),
                 jax.ShapeDtypeStruct((hc, nk, bsz * s), F32)],
      compiler_params=_params("arbitrary", "arbitrary"),
      name="merge_out",
  )(attn, sgu, ga, gg, x, gt1, sh2, sc2, ln1_g, ln1_b, *ws, keys)


def _young_cells():
  return [(a, b) for a in range(PEER_TOPK) for b in range(PEER_TOPK) if (a + 1) * (b + 1) <= PEER_TOPK]


def _route_kernel(st_ref, ca_ref, na_ref, eb_ref, rb_ref, tv_scr, cell_scr):
  nk, tn = st_ref.shape[1], st_ref.shape[2]
  row = lax.broadcasted_iota(jnp.int32, (nk, tn), 0)
  cells = _young_cells()
  npad = cell_scr.shape[0]
  crow = lax.broadcasted_iota(jnp.int32, (npad, tn), 0)
  kf = float(PEER_TOPK)

  def top_sorted(s, half):
    rank = jnp.full((nk, tn), float(nk), F32)
    for r in range(PEER_TOPK):
      mx = jnp.max(s, axis=0, keepdims=True)
      idx = jnp.min(jnp.where(s == mx, row, nk), axis=0, keepdims=True)
      pick = row == idx
      rank = jnp.where(pick, float(r), rank)
      s = jnp.where(pick, -jnp.inf, s)
      tv_scr[half, r:r + 1, :] = mx
    return rank

  def head(h, carry):
    rank0 = top_sorted(st_ref[2 * h], 0)
    rank1 = top_sorted(st_ref[2 * h + 1], 1)
    tv0 = tv_scr[0]
    tv1 = tv_scr[1]
    for ci, (a, b) in enumerate(cells):
      cell_scr[ci:ci + 1, :] = tv0[a:a + 1] + tv1[b:b + 1]
    if npad > len(cells):
      cell_scr[len(cells):, :] = jnp.full((npad - len(cells), tn), -jnp.inf, F32)
    cand = cell_scr[...]
    top = tv0[0:1] + tv1[0:1]
    picked = jnp.zeros((npad, tn), F32)
    z = jnp.zeros((1, tn), F32)
    for _ in range(PEER_TOPK):
      mx = jnp.max(cand, axis=0, keepdims=True)
      idx = jnp.min(jnp.where(cand == mx, crow, npad), axis=0, keepdims=True)
      pick = crow == idx
      picked = jnp.where(pick, 1.0, picked)
      cand = jnp.where(pick, -jnp.inf, cand)
      z = z + jnp.exp(mx - top)
    cell_scr[...] = picked
    na = jnp.zeros((nk, tn), F32)
    start = 0
    for a in range(PEER_TOPK):
      cnt = sum(1 for (aa, _) in cells if aa == a)
      n_a = jnp.sum(cell_scr[start:start + cnt, :], axis=0, keepdims=True)
      na = jnp.where(rank0 == float(a), n_a, na)
      start += cnt
    ca_ref[h] = jnp.where(rank0 < kf, 0.5 * jnp.exp(st_ref[2 * h] - tv0[0:1]) / z, 0.0)
    na_ref[h] = na
    eb_ref[h] = jnp.exp(st_ref[2 * h + 1] - tv1[0:1]).astype(eb_ref.dtype)
    rb_ref[h] = rank1.astype(rb_ref.dtype)
    return carry

  lax.fori_loop(0, ca_ref.shape[0], head, 0)


def _peer_route(st):
  hc, nk, t = st.shape
  heads = hc // 2
  tn = LANES
  assert t % tn == 0
  npad = -(-len(_young_cells()) // 8) * 8
  out = lambda dt: jax.ShapeDtypeStruct((heads, nk, t), dt)
  ospec = pl.BlockSpec((heads, nk, tn), lambda i: (0, 0, i))
  return pl.pallas_call(
      _route_kernel,
      grid=(t // tn,),
      in_specs=[pl.BlockSpec((hc, nk, tn), lambda i: (0, 0, i))],
      out_specs=[ospec] * 4,
      out_shape=[out(F32), out(F32), out(BF16), out(BF16)],
      scratch_shapes=[pltpu.VMEM((2, PEER_TOPK, tn), F32), pltpu.VMEM((npad, tn), F32)],
      compiler_params=_params("arbitrary"),
      name="peer_route",
  )(st)


PEER_SUB_EXPERTS = 512
PEER_UNIT_PIECES = 1


BF16_ROWS = 16


def _peer_dense_kernel(h2t_ref, u_ref, v_ref, ca_ref, na_ref, eb_ref, rb_ref, x1_ref, gt2_ref, g2_ref, b2_ref,
                       o_ref, acc_scr, at_scr, p_scr, pt_scr, *, alpha):
  e = pl.program_id(1)
  ec = u_ref.shape[0]
  tn = h2t_ref.shape[1]
  heads, nk = eb_ref.shape[0], eb_ref.shape[1]
  sub = PEER_SUB_EXPERTS
  n_sub = ec // sub

  @pl.when(e == 0)
  def _():
    acc_scr[...] = jnp.zeros_like(acc_scr)

  parts = PEER_UNIT_PIECES
  piece = sub // parts
  d_model = v_ref.shape[1]
  qcols = d_model // parts

  def pre_activations(sc, part):
    rows = slice(part * piece, (part + 1) * piece)
    at_scr[sc % 2, rows, :] = jnp.dot(u_ref[sc * sub + part * piece:sc * sub + (part + 1) * piece, :], h2t_ref[...],
                                      preferred_element_type=F32)

  def packed_rows(ref, h, r, ls):
    return jnp.broadcast_to(ref[h, r:r + 1, ls], (BF16_ROWS, LANES)).astype(BF16)

  zero = jnp.zeros((), BF16)

  def gate(h, ca, na, js, ls):
    return ca[h] * jnp.minimum(jnp.maximum(na[h] - rb_ref[h, js, ls], zero), eb_ref[h, js, ls])

  def gated_activations(sc, part):
    slot = sc % 2
    for ii in range(piece // nk):
      i_local = part * (piece // nk) + ii
      r = sc * (sub // nk) + i_local
      for lc in range(tn // LANES):
        ls = slice(lc * LANES, (lc + 1) * LANES)
        ca = [packed_rows(ca_ref, h, r, ls) for h in range(heads)]
        na = [packed_rows(na_ref, h, r, ls) for h in range(heads)]
        for rc in range(nk // BF16_ROWS):
          js = slice(rc * BF16_ROWS, (rc + 1) * BF16_ROWS)
          w = gate(0, ca, na, js, ls)
          for h in range(1, heads):
            w = w + gate(h, ca, na, js, ls)
          rows = slice(i_local * nk + rc * BF16_ROWS, i_local * nk + (rc + 1) * BF16_ROWS)
          a = at_scr[slot, rows, ls]
          p_scr[slot, rows, ls] = w * (a + a * lax.erf(a * (2.0 ** -0.5))).astype(BF16)

  def transpose_gated(sc):
    pt_scr[sc % 2] = p_scr[sc % 2].T

  def values(sc, q):
    cols = slice(q * qcols, (q + 1) * qcols)
    acc_scr[:, cols] += jnp.dot(pt_scr[sc % 2], v_ref[sc * sub:(sc + 1) * sub, cols], preferred_element_type=F32)

  for part in range(parts):
    pre_activations(0, part)
  for sc in range(n_sub):
    for part in range(parts):
      if sc + 1 < n_sub:
        pre_activations(sc + 1, part)
      if sc >= 1:
        values(sc - 1, part)
      gated_activations(sc, part)
    transpose_gated(sc)
  for q in range(parts):
    values(n_sub - 1, q)

  @pl.when(e == pl.num_programs(1) - 1)
  def _():
    y = alpha * x1_ref[...] + (1.0 + gt2_ref[0]) * acc_scr[...]
    o_ref[...] = _layernorm(y) * g2_ref[...] + b2_ref[...]


def _peer_dense(h2t, u, v, ca, na, eb, rb, x1, gt2, ln2_g, ln2_b, seq, alpha):
  d, t = h2t.shape
  n_exp = u.shape[0]
  heads, nk, _ = ca.shape
  assert n_exp == nk * nk
  per_step = 16
  tn, ec = 512, per_step * nk
  assert seq % tn == 0 and n_exp % ec == 0 and ec % PEER_SUB_EXPERTS == 0 and PEER_SUB_EXPERTS % nk == 0
  ub, vb = u.astype(BF16), v.astype(BF16)
  tok = lambda width: pl.BlockSpec((tn, width), lambda i, e: (i, 0))
  expert = pl.BlockSpec((ec, d), lambda i, e: (e, 0))
  fac = pl.BlockSpec((heads, nk, tn), lambda i, e: (0, 0, i))
  fac_i = pl.BlockSpec((heads, per_step, tn), lambda i, e: (0, e, i))
  one = pl.BlockSpec((1, d), lambda i, e: (0, 0))
  return pl.pallas_call(
      functools.partial(_peer_dense_kernel, alpha=alpha),
      grid=(t // tn, n_exp // ec),
      in_specs=[pl.BlockSpec((d, tn), lambda i, e: (0, i)), expert, expert, fac_i, fac_i, fac, fac, tok(d),
                pl.BlockSpec((1, 1, d), lambda i, e: ((i * tn) // seq, 0, 0)), one, one],
      out_specs=tok(d),
      out_shape=jax.ShapeDtypeStruct((t, d), F32),
      scratch_shapes=[pltpu.VMEM((tn, d), F32), pltpu.VMEM((2, PEER_SUB_EXPERTS, tn), F32),
                      pltpu.VMEM((2, PEER_SUB_EXPERTS, tn), BF16), pltpu.VMEM((2, tn, PEER_SUB_EXPERTS), BF16)],
      compiler_params=_params("arbitrary", "arbitrary"),
      name="peer_dense",
  )(h2t, ub, vb, ca, na, eb, rb, x1, gt2, ln2_g, ln2_b)


def kernel(x, c, w_cond, b_cond, w_in, w_attn_up, w_gmlp_up, w_out, sgu_ln_g, sgu_ln_b, w_spatial, b_spatial,
           ln1_g, ln1_b, w_peer_q, peer_sub_keys, peer_u, peer_v, ln2_g, ln2_b):
  bsz, s, d = x.shape
  t = bsz * s
  depth = w_cond.shape[0]
  alpha = float((2.0 * depth) ** 0.25)
  for l in range(depth):
    mod = _cond(c, w_cond[l], b_cond[l])
    sh1, sc1, gt1, sh2, sc2, gt2 = [m[:, None, :] for m in jnp.split(mod, 6, axis=-1)]
    q, k, v, kmean, gu, gv, ga, gg = _in_proj(x, sh1, sc1, w_in[l])
    attn = _moba_attention(q, k, v, kmean)
    sgu = _sgu(gu.reshape(t, -1), gv.reshape(t, -1), sgu_ln_g[l], sgu_ln_b[l], w_spatial[l], b_spatial[l])
    x1, h2t, st = _merge_out(attn, sgu.reshape(bsz, s, -1), ga, gg, x, gt1, sh2, sc2,
                             ln1_g[l].reshape(1, d), ln1_b[l].reshape(1, d),
                             w_attn_up[l], w_gmlp_up[l], w_out[l], w_peer_q[l], peer_sub_keys[l], alpha)
    ca, na, eb, rb = _peer_route(st)
    x = _peer_dense(h2t, peer_u[l], peer_v[l], ca, na, eb, rb, x1.reshape(t, d), gt2,
                    ln2_g[l].reshape(1, d), ln2_b[l].reshape(1, d), s, alpha).reshape(bsz, s, d)
  return x
```

```python
import functools

import jax
import jax.numpy as jnp
from jax import lax
from jax.experimental import pallas as pl
from jax.experimental.pallas import tpu as pltpu

F32 = jnp.float32
BF16 = jnp.bfloat16
HIGHEST = lax.Precision.HIGHEST

N_ATTN_HEADS = 8
ATTN_HEAD_DIM = 64
MOBA_BLOCK = 256
MOBA_TOPK = 3
GMLP_GROUPS = 8
GMLP_GROUP_DIM = 64
GMLP_CHUNK = 128
PEER_HEADS = 8
PEER_N_KEYS = 128
PEER_TOPK = 16
LN_EPS = 1e-5

LANES = 128
V7X_VMEM_LIMIT_BYTES = 56 * 1024 * 1024

NEG_BIG = -1e30
LOG2E = 1.4426950408889634


def _params(*sem):
  return pltpu.CompilerParams(dimension_semantics=sem, vmem_limit_bytes=V7X_VMEM_LIMIT_BYTES)


def _layernorm(x):
  mu = jnp.mean(x, axis=-1, keepdims=True)
  d = x - mu
  var = jnp.mean(d * d, axis=-1, keepdims=True)
  return d * lax.rsqrt(var + LN_EPS)


def _gelu(x):
  return 0.5 * x * (1.0 + lax.erf(x * (2.0 ** -0.5)))


def _cond_kernel(c_ref, w_ref, b_ref, o_ref):
  cs = jax.nn.silu(c_ref[...])
  o_ref[...] = jnp.dot(cs, w_ref[...], preferred_element_type=F32, precision=HIGHEST) + b_ref[...]


def _cond(c, w, b):
  bsz, d = c.shape
  n = w.shape[1]
  rows = 8
  cp = jnp.zeros((rows, d), F32).at[:bsz].set(c)
  tn = 768 if n % 768 == 0 else n
  out = pl.pallas_call(
      _cond_kernel,
      grid=(n // tn,),
      in_specs=[
          pl.BlockSpec((rows, d), lambda j: (0, 0)),
          pl.BlockSpec((d, tn), lambda j: (0, j)),
          pl.BlockSpec((1, tn), lambda j: (0, j)),
      ],
      out_specs=pl.BlockSpec((rows, tn), lambda j: (0, j)),
      out_shape=jax.ShapeDtypeStruct((rows, n), F32),
      compiler_params=_params("arbitrary"),
      name="adaln_cond",
  )(cp, w, b.reshape(1, n))
  return out[:bsz]


def _inproj_kernel(x_ref, sh_ref, sc_ref, wq, wk, wv, wgu, wgv, wga, wgg,
                   q_ref, k_ref, v_ref, km_ref, gu_ref, gv_ref, ga_ref, gg_ref):
  h = (_layernorm(x_ref[0]) * (1.0 + sc_ref[0]) + sh_ref[0]).astype(BF16)
  dot = lambda w: jnp.dot(h, w[...], preferred_element_type=F32)
  q_ref[0] = dot(wq).T
  k = dot(wk)
  tm, width = k.shape
  km_ref[0, 0] = jnp.mean(k.reshape(tm // MOBA_BLOCK, MOBA_BLOCK, width), axis=1)
  k_ref[0] = k.astype(BF16)
  v = dot(wv)
  for kb in range(tm // MOBA_BLOCK):
    v_ref[0, kb] = v[kb * MOBA_BLOCK:(kb + 1) * MOBA_BLOCK].T.astype(BF16)
  gu_ref[0] = dot(wgu)
  gv_ref[0] = dot(wgv)
  ga_ref[0] = dot(wga)
  gg_ref[0] = dot(wgg)


def _in_proj(x, sh, sc, w_in):
  bsz, s, d = x.shape
  aw = N_ATTN_HEADS * ATTN_HEAD_DIM
  gw = GMLP_GROUPS * GMLP_GROUP_DIM
  bounds = [0, aw, 2 * aw, 3 * aw, 3 * aw + gw, 3 * aw + 2 * gw, 3 * aw + 2 * gw + d, 3 * aw + 2 * gw + 2 * d]
  assert w_in.shape[1] == bounds[-1]
  ws = [w_in[:, a:b].astype(BF16) for a, b in zip(bounds[:-1], bounds[1:])]
  tm = 512
  assert s % tm == 0 and tm % MOBA_BLOCK == 0
  nt = s // tm
  row = lambda width: pl.BlockSpec((1, tm, width), lambda b, i: (b, i, 0))
  vec = pl.BlockSpec((1, 1, d), lambda b, i: (b, 0, 0))
  wspec = lambda w: pl.BlockSpec(w.shape, lambda b, i: (0, 0))
  sds = lambda width, dt: jax.ShapeDtypeStruct((bsz, s, width), dt)
  kpt = tm // MOBA_BLOCK
  outs = pl.pallas_call(
      _inproj_kernel,
      grid=(bsz, nt),
      in_specs=[row(d), vec, vec] + [wspec(w) for w in ws],
      out_specs=[pl.BlockSpec((1, aw, tm), lambda b, i: (b, 0, i)), row(aw),
                 pl.BlockSpec((1, kpt, aw, MOBA_BLOCK), lambda b, i: (b, i, 0, 0)),
                 pl.BlockSpec((1, 1, kpt, aw), lambda b, i: (b, i, 0, 0)),
                 row(gw), row(gw), row(d), row(d)],
      out_shape=[jax.ShapeDtypeStruct((bsz, aw, s), F32), sds(aw, BF16),
                 jax.ShapeDtypeStruct((bsz, s // MOBA_BLOCK, aw, MOBA_BLOCK), BF16),
                 jax.ShapeDtypeStruct((bsz, nt, kpt, aw), F32),
                 sds(gw, F32), sds(gw, F32), sds(d, F32), sds(d, F32)],
      compiler_params=_params("arbitrary", "arbitrary"),
      name="in_proj",
  )(x, sh, sc, *ws)
  q, k, v, km, gu, gv, ga, gg = outs
  return q, k, v, km.reshape(bsz, s // MOBA_BLOCK, aw), gu, gv, ga, gg


def _alibi_slopes():
  return [2.0 ** (-8.0 * (i + 1) / N_ATTN_HEADS) for i in range(N_ATTN_HEADS)]


def _key_side_table(s):
  nb = s // MOBA_BLOCK
  assert nb + 6 <= LANES
  pos = jnp.arange(s, dtype=F32)
  blk = jnp.arange(s, dtype=jnp.int32) // MOBA_BLOCK
  onehot = (blk[:, None] == jnp.arange(nb)[None, :]).astype(F32)
  slopes = _alibi_slopes()

  def top16(a):
    bits = lax.bitcast_convert_type(a, jnp.uint32) & jnp.uint32(0xFFFF0000)
    return lax.bitcast_convert_type(bits, F32)

  tables = []
  for hp in range(N_ATTN_HEADS // 2):
    cols = [onehot]
    for h in (2 * hp, 2 * hp + 1):
      c = (slopes[h] * LOG2E) * pos
      hi = top16(c)
      mid = top16(c - hi)
      lo = c - hi - mid
      cols += [hi[:, None], mid[:, None], lo[:, None]]
    t = jnp.concatenate(cols, axis=1)
    tables.append(jnp.pad(t, ((0, 0), (0, LANES - t.shape[1]))))
  return jnp.stack(tables).astype(BF16)


ATTN_BLOCKS_PER_STEP = 4


def _attn_kernel(qt_ref, k_ref, kb_ref, vt_ref, km_ref, o_ref, m_scr, l_scr, acc_scr, sa_scr, sb_scr, *, nb):
  j = pl.program_id(2)
  tq = MOBA_BLOCK
  hd = ATTN_HEAD_DIM
  q2 = qt_ref[0]
  row = lax.broadcasted_iota(jnp.int32, (LANES, tq), 0)
  qscale = (hd ** -0.5) * LOG2E

  rhs_own, rhs_past = [], []
  for h in range(2):
    qh = jnp.where((row >= hd * h) & (row < hd * (h + 1)), q2, 0.0)
    gate = jnp.dot(km_ref[0], qh, preferred_element_type=F32, precision=HIGHEST)[:nb]
    blk = lax.broadcasted_iota(jnp.int32, (nb, tq), 0)
    g = jnp.where(blk < j, gate, -jnp.inf)
    sel = blk < 0
    for _ in range(MOBA_TOPK):
      mx = jnp.max(g, axis=0, keepdims=True)
      idx = jnp.min(jnp.where(g == mx, blk, nb), axis=0, keepdims=True)
      pick = (blk == idx) & (g > -jnp.inf)
      sel = sel | pick
      g = jnp.where(pick, -jnp.inf, g)
    rest = lax.broadcasted_iota(jnp.int32, (LANES - nb, tq), 0)
    tail = jnp.where((rest >= 3 * h) & (rest < 3 * h + 3), 1.0, 0.0)
    qb = (qh * qscale).astype(BF16)
    side_past = jnp.concatenate([jnp.where(sel, 0.0, NEG_BIG), tail], axis=0).astype(BF16)
    side_own = jnp.concatenate([jnp.where(blk == j, 0.0, NEG_BIG), tail], axis=0).astype(BF16)
    rhs_past.append(jnp.concatenate([qb, side_past], axis=0))
    rhs_own.append(jnp.concatenate([qb, side_own], axis=0))

  def keys(off, n):
    return jnp.concatenate([k_ref[0, pl.ds(off, n), :], kb_ref[0, pl.ds(off, n), :]], axis=1)

  own = pl.multiple_of(j * tq, tq)
  kr = lax.broadcasted_iota(jnp.int32, (tq, tq), 0)
  qc = lax.broadcasted_iota(jnp.int32, (tq, tq), 1)
  k_own = keys(own, tq)
  for h in range(2):
    s = jnp.where(kr <= qc, jnp.dot(k_own, rhs_own[h], preferred_element_type=F32), -jnp.inf)
    m = jnp.max(s, axis=0, keepdims=True)
    p = jnp.exp2(s - m)
    m_scr[h] = m
    l_scr[h] = jnp.sum(p, axis=0, keepdims=True)
    acc_scr[h] = jnp.dot(vt_ref[0, j, hd * h:hd * (h + 1), :], p.astype(BF16), preferred_element_type=F32)

  per = ATTN_BLOCKS_PER_STEP
  n_chunks = (j + per - 1) // per
  c_last = nb // per - 1

  def qk(c, s_ref):
    kk = keys(pl.multiple_of(c * (per * tq), per * tq), per * tq)
    for h in range(2):
      s_ref[h] = jnp.dot(kk, rhs_past[h], preferred_element_type=F32)

  def softmax_pv(c, s_ref):
    for h in range(2):
      s = s_ref[h]
      m_old = m_scr[h]
      m_new = jnp.maximum(m_old, jnp.max(s, axis=0, keepdims=True))
      alpha = jnp.exp2(m_old - m_new)
      p = jnp.exp2(s - m_new)
      m_scr[h] = m_new
      l_scr[h] = alpha * l_scr[h] + jnp.sum(p, axis=0, keepdims=True)
      p = p.astype(BF16)
      pv = sum(jnp.dot(vt_ref[0, c * per + i, hd * h:hd * (h + 1), :], p[i * tq:(i + 1) * tq],
                       preferred_element_type=F32) for i in range(per))
      acc_scr[h] = alpha * acc_scr[h] + pv

  @pl.when(n_chunks > 0)
  def _():
    qk(0, sa_scr)

  def pair(i, carry):
    qk(2 * i + 1, sb_scr)
    softmax_pv(2 * i, sa_scr)
    qk(jnp.minimum(2 * i + 2, c_last), sa_scr)
    softmax_pv(2 * i + 1, sb_scr)
    return carry

  lax.fori_loop(0, (n_chunks + 1) // 2, pair, 0)
  out_t = jnp.concatenate([acc_scr[0] / l_scr[0], acc_scr[1] / l_scr[1]], axis=0)
  o_ref[0] = out_t.T.astype(o_ref.dtype)


def _moba_attention(qt, k, vt, kmean):
  bsz, aw, s = qt.shape
  nb = s // MOBA_BLOCK
  per = ATTN_BLOCKS_PER_STEP
  assert s % MOBA_BLOCK == 0 and 2 * ATTN_HEAD_DIM == LANES and nb % (2 * per) == 0
  kb = _key_side_table(s)
  km = jnp.pad(kmean, ((0, 0), (0, LANES - nb), (0, 0)))
  tq = MOBA_BLOCK
  return pl.pallas_call(
      functools.partial(_attn_kernel, nb=nb),
      grid=(bsz, N_ATTN_HEADS // 2, nb),
      in_specs=[
          pl.BlockSpec((1, LANES, tq), lambda b, hp, j: (b, hp, j)),
          pl.BlockSpec((1, s, LANES), lambda b, hp, j: (b, 0, hp)),
          pl.BlockSpec((1, s, LANES), lambda b, hp, j: (hp, 0, 0)),
          pl.BlockSpec((1, nb, LANES, tq), lambda b, hp, j: (b, 0, hp, 0)),
          pl.BlockSpec((1, LANES, LANES), lambda b, hp, j: (b, 0, hp)),
      ],
      out_specs=pl.BlockSpec((1, tq, LANES), lambda b, hp, j: (b, j, hp)),
      out_shape=jax.ShapeDtypeStruct((bsz, s, aw), BF16),
      scratch_shapes=[pltpu.VMEM((2, 1, tq), F32), pltpu.VMEM((2, 1, tq), F32),
                      pltpu.VMEM((2, ATTN_HEAD_DIM, tq), F32),
                      pltpu.VMEM((2, per * tq, tq), F32), pltpu.VMEM((2, per * tq, tq), F32)],
      compiler_params=_params("arbitrary", "arbitrary", "arbitrary"),
      name="moba_attn",
  )(qt, k, kb, vt, km)


def _split3(x):
  hi = x.astype(BF16)
  r = x - hi.astype(F32)
  mid = r.astype(BF16)
  lo = (r - mid.astype(F32)).astype(BF16)
  return hi, mid, lo


def _sgu_kernel(gu_ref, gv_ref, avg_ref, lng_ref, lnb_ref, wsp_ref, bsp_ref, o_ref):
  tm, width = gu_ref.shape
  avg = avg_ref[...]

  def group_mean(a):
    return sum(jnp.dot(p, avg, preferred_element_type=F32) for p in _split3(a))

  v = _gelu(gv_ref[...])
  d = v - group_mean(v)
  vv = (d * lax.rsqrt(group_mean(d * d) + LN_EPS) * lng_ref[...] + lnb_ref[...]).astype(BF16)
  u = _gelu(gu_ref[...])

  rows = lax.broadcasted_iota(jnp.int32, (GMLP_CHUNK, GMLP_CHUNK), 0)
  cols = lax.broadcasted_iota(jnp.int32, (GMLP_CHUNK, GMLP_CHUNK), 1)
  grp = lax.broadcasted_iota(jnp.int32, (GMLP_CHUNK, width), 1) // GMLP_GROUP_DIM
  ws = [jnp.where(rows >= cols, wsp_ref[g], 0.0).astype(BF16) for g in range(GMLP_GROUPS)]
  for ck in range(tm // GMLP_CHUNK):
    sl = slice(ck * GMLP_CHUNK, (ck + 1) * GMLP_CHUNK)
    vc = vv[sl]
    sv = bsp_ref[...]
    for g in range(GMLP_GROUPS):
      sv = sv + jnp.where(grp == g, jnp.dot(ws[g], vc, preferred_element_type=F32), 0.0)
    o_ref[sl, :] = (u[sl] * sv).astype(o_ref.dtype)


def _sgu(gu, gv, ln_g, ln_b, w_spatial, b_spatial):
  t, width = gu.shape
  tm = 512
  assert t % tm == 0 and tm % GMLP_CHUNK == 0 and width == GMLP_GROUPS * GMLP_GROUP_DIM
  ch = jnp.arange(width) // GMLP_GROUP_DIM
  avg = ((ch[:, None] == ch[None, :]).astype(F32) / GMLP_GROUP_DIM).astype(BF16)
  bsp = jnp.repeat(b_spatial.T, GMLP_GROUP_DIM, axis=1)
  row = pl.BlockSpec((tm, width), lambda i: (i, 0))
  full = lambda a: pl.BlockSpec(a.shape, lambda i: (0,) * a.ndim)
  args = (gu, gv, avg, ln_g.reshape(1, width), ln_b.reshape(1, width), w_spatial, bsp)
  return pl.pallas_call(
      _sgu_kernel,
      grid=(t // tm,),
      in_specs=[row, row] + [full(a) for a in args[2:]],
      out_specs=row,
      out_shape=jax.ShapeDtypeStruct((t, width), BF16),
      compiler_params=_params("arbitrary"),
      name="sgu",
  )(*args)


def _merge_kernel(attn_ref, sgu_ref, ga_ref, gg_ref, x_ref, gt1_ref, sh2_ref, sc2_ref, g1_ref, b1_ref,
                  wa_ref, wg_ref, wo_ref, wq_ref, keys_ref, x1_ref, h2t_ref, st_ref, *, alpha):
  a = jnp.dot(attn_ref[0], wa_ref[...], preferred_element_type=F32)
  g = jnp.dot(sgu_ref[0], wg_ref[...], preferred_element_type=F32)
  merged = jax.nn.sigmoid(ga_ref[0]) * a + jax.nn.sigmoid(gg_ref[0]) * g
  out = jnp.dot(merged.astype(BF16), wo_ref[...], preferred_element_type=F32)
  x1 = _layernorm(alpha * x_ref[0] + (1.0 + gt1_ref[0]) * out) * g1_ref[...] + b1_ref[...]
  x1_ref[0] = x1
  h2f = _layernorm(x1) * (1.0 + sc2_ref[0]) + sh2_ref[0]
  h2t_ref[...] = h2f.T.astype(BF16)
  h2 = h2f.astype(BF16)
  qp = jnp.dot(h2, wq_ref[...], preferred_element_type=F32).astype(BF16)
  half = keys_ref.shape[2]
  for hc in range(keys_ref.shape[0]):
    st_ref[hc] = lax.dot_general(keys_ref[hc], qp[:, hc * half:(hc + 1) * half],
                                 (((1,), (1,)), ((), ())), preferred_element_type=F32)


def _merge_out(attn, sgu, ga, gg, x, gt1, sh2, sc2, ln1_g, ln1_b, w_attn_up, w_gmlp_up, w_out, w_peer_q,
               sub_keys, alpha):
  bsz, s, d = x.shape
  tm = 256
  assert s % tm == 0
  nt = s // tm
  hc, nk, half = sub_keys.shape[0] * sub_keys.shape[1], sub_keys.shape[2], sub_keys.shape[3]
  keys = sub_keys.reshape(hc, nk, half).astype(BF16)
  ws = [w.astype(BF16) for w in (w_attn_up, w_gmlp_up, w_out, w_peer_q)]
  assert w_peer_q.shape[1] == hc * half
  row = lambda width: pl.BlockSpec((1, tm, width), lambda b, i: (b, i, 0))
  vec = pl.BlockSpec((1, 1, d), lambda b, i: (b, 0, 0))
  full = lambda a: pl.BlockSpec(a.shape, lambda b, i: (0,) * a.ndim)
  aw = attn.shape[-1]
  return pl.pallas_call(
      functools.partial(_merge_kernel, alpha=alpha),
      grid=(bsz, nt),
      in_specs=[row(aw), row(sgu.shape[-1]), row(d), row(d), row(d), vec, vec, vec,
                full(ln1_g), full(ln1_b)] + [full(w) for w in ws] + [full(keys)],
      out_specs=[row(d), pl.BlockSpec((d, tm), lambda b, i: (0, b * nt + i)),
                 pl.BlockSpec((hc, nk, tm), lambda b, i: (0, 0, b * nt + i))],
      out_shape=[jax.ShapeDtypeStruct((bsz, s, d), F32), jax.ShapeDtypeStruct((d, bsz * s), BF16),
                 jax.ShapeDtypeStruct((hc, nk, bsz * s), F32)],
      compiler_params=_params("arbitrary", "arbitrary"),
      name="merge_out",
  )(attn, sgu, ga, gg, x, gt1, sh2, sc2, ln1_g, ln1_b, *ws, keys)


def _young_cells():
  return [(a, b) for a in range(PEER_TOPK) for b in range(PEER_TOPK) if (a + 1) * (b + 1) <= PEER_TOPK]


def _route_kernel(st_ref, ca_ref, na_ref, eb_ref, rb_ref, tv_scr, cell_scr):
  nk, tn = st_ref.shape[1], st_ref.shape[2]
  row = lax.broadcasted_iota(jnp.int32, (nk, tn), 0)
  cells = _young_cells()
  npad = cell_scr.shape[0]
  crow = lax.broadcasted_iota(jnp.int32, (npad, tn), 0)
  kf = float(PEER_TOPK)

  def top_sorted(s, half):
    rank = jnp.full((nk, tn), float(nk), F32)
    for r in range(PEER_TOPK):
      mx = jnp.max(s, axis=0, keepdims=True)
      idx = jnp.min(jnp.where(s == mx, row, nk), axis=0, keepdims=True)
      pick = row == idx
      rank = jnp.where(pick, float(r), rank)
      s = jnp.where(pick, -jnp.inf, s)
      tv_scr[half, r:r + 1, :] = mx
    return rank

  def top_values(s, n):
    vals = []
    for _ in range(n):
      mx = jnp.max(s, axis=0, keepdims=True)
      vals.append(mx)
      s = jnp.where(s == mx, -jnp.inf, s)
    return vals

  def count_ge(s, thr):
    return jnp.sum(jnp.where(s >= thr, 1.0, 0.0), axis=0, keepdims=True)

  def head_no_ties(h):
    n = PEER_TOPK
    s0, s1 = st_ref[2 * h], st_ref[2 * h + 1]
    tv0, tv1 = top_values(s0, n), top_values(s1, n)
    ok = (count_ge(s0, tv0[-1]) == float(n)) & (count_ge(s1, tv1[-1]) == float(n))
    for ci, (a, b) in enumerate(cells):
      cell_scr[ci:ci + 1, :] = tv0[a] + tv1[b]
    if npad > len(cells):
      cell_scr[len(cells):, :] = jnp.full((npad - len(cells), tn), -jnp.inf, F32)
    cand = cell_scr[...]
    cv = top_values(cand, n)
    ok = ok & (count_ge(cand, cv[-1]) == float(n))
    z = jnp.ones((1, tn), F32)
    for r in range(1, PEER_TOPK):
      z = z + jnp.exp(cv[r] - cv[0])
    cell_scr[...] = jnp.where(cand >= cv[PEER_TOPK - 1], 1.0, 0.0)
    na = jnp.zeros((nk, tn), F32)
    start = 0
    for a in range(PEER_TOPK):
      cnt = sum(1 for (aa, _) in cells if aa == a)
      n_a = jnp.sum(cell_scr[start:start + cnt, :], axis=0, keepdims=True)
      na = jnp.where(s0 == tv0[a], n_a, na)
      start += cnt
    rb = jnp.zeros((nk, tn), F32)
    for r in range(PEER_TOPK):
      rb = rb + jnp.where(tv1[r] > s1, 1.0, 0.0)
    ca_ref[h] = jnp.where(s0 >= tv0[PEER_TOPK - 1], 0.5 * jnp.exp(s0 - tv0[0]) / z, 0.0)
    na_ref[h] = na
    eb_ref[h] = jnp.exp(s1 - tv1[0]).astype(eb_ref.dtype)
    rb_ref[h] = rb.astype(rb_ref.dtype)
    okf = jnp.broadcast_to(jnp.where(ok, 1.0, 0.0), (8, tn))
    return jnp.min(jnp.min(okf, axis=1, keepdims=True), axis=0, keepdims=True)[0, 0] > 0.5

  def head(h, carry):
    no_ties = head_no_ties(h)

    @pl.when(jnp.logical_not(no_ties))
    def _():
      head_exact(h)

    return carry

  def head_exact(h):
    rank0 = top_sorted(st_ref[2 * h], 0)
    rank1 = top_sorted(st_ref[2 * h + 1], 1)
    tv0 = tv_scr[0]
    tv1 = tv_scr[1]
    for ci, (a, b) in enumerate(cells):
      cell_scr[ci:ci + 1, :] = tv0[a:a + 1] + tv1[b:b + 1]
    if npad > len(cells):
      cell_scr[len(cells):, :] = jnp.full((npad - len(cells), tn), -jnp.inf, F32)
    cand = cell_scr[...]
    top = tv0[0:1] + tv1[0:1]
    picked = jnp.zeros((npad, tn), F32)
    z = jnp.zeros((1, tn), F32)
    for _ in range(PEER_TOPK):
      mx = jnp.max(cand, axis=0, keepdims=True)
      idx = jnp.min(jnp.where(cand == mx, crow, npad), axis=0, keepdims=True)
      pick = crow == idx
      picked = jnp.where(pick, 1.0, picked)
      cand = jnp.where(pick, -jnp.inf, cand)
      z = z + jnp.exp(mx - top)
    cell_scr[...] = picked
    na = jnp.zeros((nk, tn), F32)
    start = 0
    for a in range(PEER_TOPK):
      cnt = sum(1 for (aa, _) in cells if aa == a)
      n_a = jnp.sum(cell_scr[start:start + cnt, :], axis=0, keepdims=True)
      na = jnp.where(rank0 == float(a), n_a, na)
      start += cnt
    ca_ref[h] = jnp.where(rank0 < kf, 0.5 * jnp.exp(st_ref[2 * h] - tv0[0:1]) / z, 0.0)
    na_ref[h] = na
    eb_ref[h] = jnp.exp(st_ref[2 * h + 1] - tv1[0:1]).astype(eb_ref.dtype)
    rb_ref[h] = rank1.astype(rb_ref.dtype)

  lax.fori_loop(0, ca_ref.shape[0], head, 0)


def _peer_route(st):
  hc, nk, t = st.shape
  heads = hc // 2
  tn = LANES
  assert t % tn == 0
  npad = -(-len(_young_cells()) // 8) * 8
  out = lambda dt: jax.ShapeDtypeStruct((heads, nk, t), dt)
  ospec = pl.BlockSpec((heads, nk, tn), lambda i: (0, 0, i))
  return pl.pallas_call(
      _route_kernel,
      grid=(t // tn,),
      in_specs=[pl.BlockSpec((hc, nk, tn), lambda i: (0, 0, i))],
      out_specs=[ospec] * 4,
      out_shape=[out(F32), out(F32), out(BF16), out(BF16)],
      scratch_shapes=[pltpu.VMEM((2, PEER_TOPK, tn), F32), pltpu.VMEM((npad, tn), F32)],
      compiler_params=_params("arbitrary"),
      name="peer_route",
  )(st)


PEER_SUB_EXPERTS = 512
PEER_UNIT_PIECES = 1

BF16_ROWS = 16


def _peer_dense_kernel(h2t_ref, u_ref, v_ref, ca_ref, na_ref, eb_ref, rb_ref, x1_ref, gt2_ref, g2_ref, b2_ref,
                       o_ref, acc_scr, at_scr, p_scr, pt_scr, *, alpha):
  e = pl.program_id(1)
  ec = u_ref.shape[0]
  tn = h2t_ref.shape[1]
  heads, nk = eb_ref.shape[0], eb_ref.shape[1]
  sub = PEER_SUB_EXPERTS
  n_sub = ec // sub

  @pl.when(e == 0)
  def _():
    acc_scr[...] = jnp.zeros_like(acc_scr)

  parts = PEER_UNIT_PIECES
  piece = sub // parts
  d_model = v_ref.shape[1]
  qcols = d_model // parts

  def pre_activations(sc, part):
    rows = slice(part * piece, (part + 1) * piece)
    at_scr[sc % 2, rows, :] = jnp.dot(u_ref[sc * sub + part * piece:sc * sub + (part + 1) * piece, :], h2t_ref[...],
                                      preferred_element_type=F32)

  def packed_rows(ref, h, r, ls):
    return jnp.broadcast_to(ref[h, r:r + 1, ls], (BF16_ROWS, LANES)).astype(BF16)

  zero = jnp.zeros((), BF16)

  def gate(ca, na, rb, eb):
    return ca * jnp.minimum(jnp.maximum(na - rb, zero), eb)

  def gated_activations(sc, part):
    slot = sc % 2
    for ii in range(piece // nk):
      i_local = part * (piece // nk) + ii
      r = sc * (sub // nk) + i_local
      for lc in range(tn // LANES):
        ls = slice(lc * LANES, (lc + 1) * LANES)
        ca = [packed_rows(ca_ref, h, r, ls) for h in range(heads)]
        na = [packed_rows(na_ref, h, r, ls) for h in range(heads)]
        for rc in range(nk // BF16_ROWS):
          js = slice(rc * BF16_ROWS, (rc + 1) * BF16_ROWS)
          w = gate(ca[0], na[0], rb_ref[0, js, ls], eb_ref[0, js, ls])
          for h in range(1, heads):
            w = w + gate(ca[h], na[h], rb_ref[h, js, ls], eb_ref[h, js, ls])
          rows = slice(i_local * nk + rc * BF16_ROWS, i_local * nk + (rc + 1) * BF16_ROWS)
          a = at_scr[slot, rows, ls]
          p_scr[slot, rows, ls] = w * (a + a * lax.erf(a * (2.0 ** -0.5))).astype(BF16)

  def transpose_gated(sc):
    pt_scr[sc % 2] = p_scr[sc % 2].T

  def values(sc, q):
    cols = slice(q * qcols, (q + 1) * qcols)
    acc_scr[:, cols] += jnp.dot(pt_scr[sc % 2], v_ref[sc * sub:(sc + 1) * sub, cols], preferred_element_type=F32)

  for part in range(parts):
    pre_activations(0, part)
  for sc in range(n_sub):
    for part in range(parts):
      if sc + 1 < n_sub:
        pre_activations(sc + 1, part)
      if sc >= 1:
        values(sc - 1, part)
      gated_activations(sc, part)
    transpose_gated(sc)
  for q in range(parts):
    values(n_sub - 1, q)

  @pl.when(e == pl.num_programs(1) - 1)
  def _():
    y = alpha * x1_ref[...] + (1.0 + gt2_ref[0]) * acc_scr[...]
    o_ref[...] = _layernorm(y) * g2_ref[...] + b2_ref[...]


def _peer_dense(h2t, u, v, ca, na, eb, rb, x1, gt2, ln2_g, ln2_b, seq, alpha):
  d, t = h2t.shape
  n_exp = u.shape[0]
  heads, nk, _ = ca.shape
  assert n_exp == nk * nk
  per_step = 16
  tn, ec = 512, per_step * nk
  assert seq % tn == 0 and n_exp % ec == 0 and ec % PEER_SUB_EXPERTS == 0 and PEER_SUB_EXPERTS % nk == 0
  ub, vb = u.astype(BF16), v.astype(BF16)
  tok = lambda width: pl.BlockSpec((tn, width), lambda i, e: (i, 0))
  expert = pl.BlockSpec((ec, d), lambda i, e: (e, 0))
  fac = pl.BlockSpec((heads, nk, tn), lambda i, e: (0, 0, i))
  fac_i = pl.BlockSpec((heads, per_step, tn), lambda i, e: (0, e, i))
  one = pl.BlockSpec((1, d), lambda i, e: (0, 0))
  return pl.pallas_call(
      functools.partial(_peer_dense_kernel, alpha=alpha),
      grid=(t // tn, n_exp // ec),
      in_specs=[pl.BlockSpec((d, tn), lambda i, e: (0, i)), expert, expert, fac_i, fac_i, fac, fac, tok(d),
                pl.BlockSpec((1, 1, d), lambda i, e: ((i * tn) // seq, 0, 0)), one, one],
      out_specs=tok(d),
      out_shape=jax.ShapeDtypeStruct((t, d), F32),
      scratch_shapes=[pltpu.VMEM((tn, d), F32), pltpu.VMEM((2, PEER_SUB_EXPERTS, tn), F32),
                      pltpu.VMEM((2, PEER_SUB_EXPERTS, tn), BF16), pltpu.VMEM((2, tn, PEER_SUB_EXPERTS), BF16)],
      compiler_params=_params("arbitrary", "arbitrary"),
      name="peer_dense",
  )(h2t, ub, vb, ca, na, eb, rb, x1, gt2, ln2_g, ln2_b)


def kernel(x, c, w_cond, b_cond, w_in, w_attn_up, w_gmlp_up, w_out, sgu_ln_g, sgu_ln_b, w_spatial, b_spatial,
           ln1_g, ln1_b, w_peer_q, peer_sub_keys, peer_u, peer_v, ln2_g, ln2_b):
  bsz, s, d = x.shape
  t = bsz * s
  depth = w_cond.shape[0]
  alpha = float((2.0 * depth) ** 0.25)
  for l in range(depth):
    mod = _cond(c, w_cond[l], b_cond[l])
    sh1, sc1, gt1, sh2, sc2, gt2 = [m[:, None, :] for m in jnp.split(mod, 6, axis=-1)]
    q, k, v, kmean, gu, gv, ga, gg = _in_proj(x, sh1, sc1, w_in[l])
    attn = _moba_attention(q, k, v, kmean)
    sgu = _sgu(gu.reshape(t, -1), gv.reshape(t, -1), sgu_ln_g[l], sgu_ln_b[l], w_spatial[l], b_spatial[l])
    x1, h2t, st = _merge_out(attn, sgu.reshape(bsz, s, -1), ga, gg, x, gt1, sh2, sc2,
                             ln1_g[l].reshape(1, d), ln1_b[l].reshape(1, d),
                             w_attn_up[l], w_gmlp_up[l], w_out[l], w_peer_q[l], peer_sub_keys[l], alpha)
    ca, na, eb, rb = _peer_route(st)
    x = _peer_dense(h2t, peer_u[l], peer_v[l], ca, na, eb, rb, x1.reshape(t, d), gt2,
                    ln2_g[l].reshape(1, d), ln2_b[l].reshape(1, d), s, alpha).reshape(bsz, s, d)
  return x
```

```python
import functools

import jax
import jax.numpy as jnp
from jax import lax
from jax.experimental import pallas as pl
from jax.experimental.pallas import tpu as pltpu

F32 = jnp.float32
BF16 = jnp.bfloat16
HIGHEST = lax.Precision.HIGHEST

N_ATTN_HEADS = 8
ATTN_HEAD_DIM = 64
MOBA_BLOCK = 256
MOBA_TOPK = 3
GMLP_GROUPS = 8
GMLP_GROUP_DIM = 64
GMLP_CHUNK = 128
PEER_HEADS = 8
PEER_N_KEYS = 128
PEER_TOPK = 16
LN_EPS = 1e-5

LANES = 128
V7X_VMEM_LIMIT_BYTES = 56 * 1024 * 1024

NEG_BIG = -1e30
LOG2E = 1.4426950408889634


def _params(*sem):
  return pltpu.CompilerParams(dimension_semantics=sem, vmem_limit_bytes=V7X_VMEM_LIMIT_BYTES)


def _layernorm(x):
  mu = jnp.mean(x, axis=-1, keepdims=True)
  d = x - mu
  var = jnp.mean(d * d, axis=-1, keepdims=True)
  return d * lax.rsqrt(var + LN_EPS)


def _gelu(x):
  return 0.5 * x * (1.0 + lax.erf(x * (2.0 ** -0.5)))


def _cond_kernel(c_ref, w_ref, b_ref, o_ref):
  cs = jax.nn.silu(c_ref[...])
  o_ref[...] = jnp.dot(cs, w_ref[...], preferred_element_type=F32, precision=HIGHEST) + b_ref[...]


def _cond(c, w, b):
  bsz, d = c.shape
  n = w.shape[1]
  rows = 8
  cp = jnp.zeros((rows, d), F32).at[:bsz].set(c)
  tn = 768 if n % 768 == 0 else n
  out = pl.pallas_call(
      _cond_kernel,
      grid=(n // tn,),
      in_specs=[
          pl.BlockSpec((rows, d), lambda j: (0, 0)),
          pl.BlockSpec((d, tn), lambda j: (0, j)),
          pl.BlockSpec((1, tn), lambda j: (0, j)),
      ],
      out_specs=pl.BlockSpec((rows, tn), lambda j: (0, j)),
      out_shape=jax.ShapeDtypeStruct((rows, n), F32),
      compiler_params=_params("arbitrary"),
      name="adaln_cond",
  )(cp, w, b.reshape(1, n))
  return out[:bsz]


def _inproj_kernel(x_ref, sh_ref, sc_ref, wq, wk, wv, wgu, wgv, wga, wgg,
                   q_ref, k_ref, v_ref, km_ref, gu_ref, gv_ref, ga_ref, gg_ref):
  h = (_layernorm(x_ref[0]) * (1.0 + sc_ref[0]) + sh_ref[0]).astype(BF16)
  dot = lambda w: jnp.dot(h, w[...], preferred_element_type=F32)
  q_ref[0] = dot(wq).T
  k = dot(wk)
  tm, width = k.shape
  km_ref[0, 0] = jnp.mean(k.reshape(tm // MOBA_BLOCK, MOBA_BLOCK, width), axis=1)
  k_ref[0] = k.astype(BF16)
  v = dot(wv)
  for kb in range(tm // MOBA_BLOCK):
    v_ref[0, kb] = v[kb * MOBA_BLOCK:(kb + 1) * MOBA_BLOCK].T.astype(BF16)
  gu_ref[0] = dot(wgu)
  gv_ref[0] = dot(wgv)
  ga_ref[0] = dot(wga)
  gg_ref[0] = dot(wgg)


def _in_proj(x, sh, sc, w_in):
  bsz, s, d = x.shape
  aw = N_ATTN_HEADS * ATTN_HEAD_DIM
  gw = GMLP_GROUPS * GMLP_GROUP_DIM
  bounds = [0, aw, 2 * aw, 3 * aw, 3 * aw + gw, 3 * aw + 2 * gw, 3 * aw + 2 * gw + d, 3 * aw + 2 * gw + 2 * d]
  assert w_in.shape[1] == bounds[-1]
  ws = [w_in[:, a:b].astype(BF16) for a, b in zip(bounds[:-1], bounds[1:])]
  tm = 512
  assert s % tm == 0 and tm % MOBA_BLOCK == 0
  nt = s // tm
  row = lambda width: pl.BlockSpec((1, tm, width), lambda b, i: (b, i, 0))
  vec = pl.BlockSpec((1, 1, d), lambda b, i: (b, 0, 0))
  wspec = lambda w: pl.BlockSpec(w.shape, lambda b, i: (0, 0))
  sds = lambda width, dt: jax.ShapeDtypeStruct((bsz, s, width), dt)
  kpt = tm // MOBA_BLOCK
  outs = pl.pallas_call(
      _inproj_kernel,
      grid=(bsz, nt),
      in_specs=[row(d), vec, vec] + [wspec(w) for w in ws],
      out_specs=[pl.BlockSpec((1, aw, tm), lambda b, i: (b, 0, i)), row(aw),
                 pl.BlockSpec((1, kpt, aw, MOBA_BLOCK), lambda b, i: (b, i, 0, 0)),
                 pl.BlockSpec((1, 1, kpt, aw), lambda b, i: (b, i, 0, 0)),
                 row(gw), row(gw), row(d), row(d)],
      out_shape=[jax.ShapeDtypeStruct((bsz, aw, s), F32), sds(aw, BF16),
                 jax.ShapeDtypeStruct((bsz, s // MOBA_BLOCK, aw, MOBA_BLOCK), BF16),
                 jax.ShapeDtypeStruct((bsz, nt, kpt, aw), F32),
                 sds(gw, F32), sds(gw, F32), sds(d, F32), sds(d, F32)],
      compiler_params=_params("arbitrary", "arbitrary"),
      name="in_proj",
  )(x, sh, sc, *ws)
  q, k, v, km, gu, gv, ga, gg = outs
  return q, k, v, km.reshape(bsz, s // MOBA_BLOCK, aw), gu, gv, ga, gg


def _alibi_slopes():
  return [2.0 ** (-8.0 * (i + 1) / N_ATTN_HEADS) for i in range(N_ATTN_HEADS)]


def _key_side_table(s):
  nb = s // MOBA_BLOCK
  assert nb + 6 <= LANES
  pos = jnp.arange(s, dtype=F32)
  blk = jnp.arange(s, dtype=jnp.int32) // MOBA_BLOCK
  onehot = (blk[:, None] == jnp.arange(nb)[None, :]).astype(F32)
  slopes = _alibi_slopes()

  def top16(a):
    bits = lax.bitcast_convert_type(a, jnp.uint32) & jnp.uint32(0xFFFF0000)
    return lax.bitcast_convert_type(bits, F32)

  tables = []
  for hp in range(N_ATTN_HEADS // 2):
    cols = [onehot]
    for h in (2 * hp, 2 * hp + 1):
      c = (slopes[h] * LOG2E) * pos
      hi = top16(c)
      mid = top16(c - hi)
      lo = c - hi - mid
      cols += [hi[:, None], mid[:, None], lo[:, None]]
    t = jnp.concatenate(cols, axis=1)
    tables.append(jnp.pad(t, ((0, 0), (0, LANES - t.shape[1]))))
  return jnp.stack(tables).astype(BF16)


ATTN_BLOCKS_PER_STEP = 4


def _attn_kernel(qt_ref, k_ref, kb_ref, vt_ref, km_ref, o_ref, m_scr, l_scr, acc_scr, sa_scr, sb_scr, *, nb):
  j = pl.program_id(2)
  tq = MOBA_BLOCK
  hd = ATTN_HEAD_DIM
  q2 = qt_ref[0]
  row = lax.broadcasted_iota(jnp.int32, (LANES, tq), 0)
  qscale = (hd ** -0.5) * LOG2E

  rhs_own, rhs_past = [], []
  for h in range(2):
    qh = jnp.where((row >= hd * h) & (row < hd * (h + 1)), q2, 0.0)
    gate = jnp.dot(km_ref[0], qh, preferred_element_type=F32, precision=HIGHEST)
    blk = lax.broadcasted_iota(jnp.int32, (nb, tq), 0)
    g = jnp.where(blk < j, gate, -jnp.inf)
    sel = blk < 0
    for _ in range(MOBA_TOPK):
      mx = jnp.max(g, axis=0, keepdims=True)
      idx = jnp.min(jnp.where(g == mx, blk, nb), axis=0, keepdims=True)
      pick = (blk == idx) & (g > -jnp.inf)
      sel = sel | pick
      g = jnp.where(pick, -jnp.inf, g)
    rest = lax.broadcasted_iota(jnp.int32, (LANES - nb, tq), 0)
    tail = jnp.where((rest >= 3 * h) & (rest < 3 * h + 3), 1.0, 0.0)
    qb = (qh * qscale).astype(BF16)
    side_past = jnp.concatenate([jnp.where(sel, 0.0, NEG_BIG), tail], axis=0).astype(BF16)
    side_own = jnp.concatenate([jnp.where(blk == j, 0.0, NEG_BIG), tail], axis=0).astype(BF16)
    rhs_past.append(jnp.concatenate([qb, side_past], axis=0))
    rhs_own.append(jnp.concatenate([qb, side_own], axis=0))

  def keys(off, n):
    return jnp.concatenate([k_ref[0, pl.ds(off, n), :], kb_ref[0, pl.ds(off, n), :]], axis=1)

  own = pl.multiple_of(j * tq, tq)
  kr = lax.broadcasted_iota(jnp.int32, (tq, tq), 0)
  qc = lax.broadcasted_iota(jnp.int32, (tq, tq), 1)
  k_own = keys(own, tq)
  for h in range(2):
    s = jnp.where(kr <= qc, jnp.dot(k_own, rhs_own[h], preferred_element_type=F32), -jnp.inf)
    m = jnp.max(s, axis=0, keepdims=True)
    p = jnp.exp2(s - m)
    m_scr[h] = m
    l_scr[h] = jnp.sum(p, axis=0, keepdims=True)
    acc_scr[h] = jnp.dot(vt_ref[0, j, hd * h:hd * (h + 1), :], p.astype(BF16), preferred_element_type=F32)

  per = ATTN_BLOCKS_PER_STEP
  n_chunks = (j + per - 1) // per
  c_last = nb // per - 1

  def qk(c, s_ref):
    kk = keys(pl.multiple_of(c * (per * tq), per * tq), per * tq)
    for h in range(2):
      s_ref[h] = jnp.dot(kk, rhs_past[h], preferred_element_type=F32)

  def softmax_pv(c, s_ref):
    for h in range(2):
      s = s_ref[h]
      m_old = m_scr[h]
      m_new = jnp.maximum(m_old, jnp.max(s, axis=0, keepdims=True))
      alpha = jnp.exp2(m_old - m_new)
      p = jnp.exp2(s - m_new)
      m_scr[h] = m_new
      l_scr[h] = alpha * l_scr[h] + jnp.sum(p, axis=0, keepdims=True)
      p = p.astype(BF16)
      pv = sum(jnp.dot(vt_ref[0, c * per + i, hd * h:hd * (h + 1), :], p[i * tq:(i + 1) * tq],
                       preferred_element_type=F32) for i in range(per))
      acc_scr[h] = alpha * acc_scr[h] + pv

  @pl.when(n_chunks > 0)
  def _():
    qk(0, sa_scr)

  def pair(i, carry):
    qk(2 * i + 1, sb_scr)
    softmax_pv(2 * i, sa_scr)
    qk(jnp.minimum(2 * i + 2, c_last), sa_scr)
    softmax_pv(2 * i + 1, sb_scr)
    return carry

  lax.fori_loop(0, (n_chunks + 1) // 2, pair, 0)
  out_t = jnp.concatenate([acc_scr[0] / l_scr[0], acc_scr[1] / l_scr[1]], axis=0)
  o_ref[0] = out_t.T.astype(o_ref.dtype)


def _moba_attention(qt, k, vt, kmean):
  bsz, aw, s = qt.shape
  nb = s // MOBA_BLOCK
  per = ATTN_BLOCKS_PER_STEP
  assert s % MOBA_BLOCK == 0 and 2 * ATTN_HEAD_DIM == LANES and nb % (2 * per) == 0
  kb = _key_side_table(s)
  tq = MOBA_BLOCK
  return pl.pallas_call(
      functools.partial(_attn_kernel, nb=nb),
      grid=(bsz, N_ATTN_HEADS // 2, nb),
      in_specs=[
          pl.BlockSpec((1, LANES, tq), lambda b, hp, j: (b, hp, j)),
          pl.BlockSpec((1, s, LANES), lambda b, hp, j: (b, 0, hp)),
          pl.BlockSpec((1, s, LANES), lambda b, hp, j: (hp, 0, 0)),
          pl.BlockSpec((1, nb, LANES, tq), lambda b, hp, j: (b, 0, hp, 0)),
          pl.BlockSpec((1, nb, LANES), lambda b, hp, j: (b, 0, hp)),
      ],
      out_specs=pl.BlockSpec((1, tq, LANES), lambda b, hp, j: (b, j, hp)),
      out_shape=jax.ShapeDtypeStruct((bsz, s, aw), BF16),
      scratch_shapes=[pltpu.VMEM((2, 1, tq), F32), pltpu.VMEM((2, 1, tq), F32),
                      pltpu.VMEM((2, ATTN_HEAD_DIM, tq), F32),
                      pltpu.VMEM((2, per * tq, tq), F32), pltpu.VMEM((2, per * tq, tq), F32)],
      compiler_params=_params("arbitrary", "arbitrary", "arbitrary"),
      name="moba_attn",
  )(qt, k, kb, vt, kmean)


def _split3(x):
  hi = x.astype(BF16)
  r = x - hi.astype(F32)
  mid = r.astype(BF16)
  lo = (r - mid.astype(F32)).astype(BF16)
  return hi, mid, lo


def _sgu_kernel(gu_ref, gv_ref, avg_ref, lng_ref, lnb_ref, wsp_ref, bsp_ref, o_ref):
  tm, width = gu_ref.shape
  avg = avg_ref[...]

  def group_mean(a):
    return sum(jnp.dot(p, avg, preferred_element_type=F32) for p in _split3(a))

  v = _gelu(gv_ref[...])
  d = v - group_mean(v)
  vv = (d * lax.rsqrt(group_mean(d * d) + LN_EPS) * lng_ref[...] + lnb_ref[...]).astype(BF16)
  u = _gelu(gu_ref[...])

  rows = lax.broadcasted_iota(jnp.int32, (GMLP_CHUNK, GMLP_CHUNK), 0)
  cols = lax.broadcasted_iota(jnp.int32, (GMLP_CHUNK, GMLP_CHUNK), 1)
  grp = lax.broadcasted_iota(jnp.int32, (GMLP_CHUNK, width), 1) // GMLP_GROUP_DIM
  ws = [jnp.where(rows >= cols, wsp_ref[g], 0.0).astype(BF16) for g in range(GMLP_GROUPS)]
  for ck in range(tm // GMLP_CHUNK):
    sl = slice(ck * GMLP_CHUNK, (ck + 1) * GMLP_CHUNK)
    vc = vv[sl]
    sv = bsp_ref[...]
    for g in range(GMLP_GROUPS):
      sv = sv + jnp.where(grp == g, jnp.dot(ws[g], vc, preferred_element_type=F32), 0.0)
    o_ref[sl, :] = (u[sl] * sv).astype(o_ref.dtype)


def _sgu(gu, gv, ln_g, ln_b, w_spatial, b_spatial):
  t, width = gu.shape
  tm = 512
  assert t % tm == 0 and tm % GMLP_CHUNK == 0 and width == GMLP_GROUPS * GMLP_GROUP_DIM
  ch = jnp.arange(width) // GMLP_GROUP_DIM
  avg = ((ch[:, None] == ch[None, :]).astype(F32) / GMLP_GROUP_DIM).astype(BF16)
  bsp = jnp.repeat(b_spatial.T, GMLP_GROUP_DIM, axis=1)
  row = pl.BlockSpec((tm, width), lambda i: (i, 0))
  full = lambda a: pl.BlockSpec(a.shape, lambda i: (0,) * a.ndim)
  args = (gu, gv, avg, ln_g.reshape(1, width), ln_b.reshape(1, width), w_spatial, bsp)
  return pl.pallas_call(
      _sgu_kernel,
      grid=(t // tm,),
      in_specs=[row, row] + [full(a) for a in args[2:]],
      out_specs=row,
      out_shape=jax.ShapeDtypeStruct((t, width), BF16),
      compiler_params=_params("arbitrary"),
      name="sgu",
  )(*args)


def _merge_kernel(attn_ref, sgu_ref, ga_ref, gg_ref, x_ref, gt1_ref, sh2_ref, sc2_ref, g1_ref, b1_ref,
                  wa_ref, wg_ref, wo_ref, wq_ref, keys_ref, x1_ref, h2t_ref, st_ref, *, alpha):
  a = jnp.dot(attn_ref[0], wa_ref[...], preferred_element_type=F32)
  g = jnp.dot(sgu_ref[0], wg_ref[...], preferred_element_type=F32)
  merged = jax.nn.sigmoid(ga_ref[0]) * a + jax.nn.sigmoid(gg_ref[0]) * g
  out = jnp.dot(merged.astype(BF16), wo_ref[...], preferred_element_type=F32)
  x1 = _layernorm(alpha * x_ref[0] + (1.0 + gt1_ref[0]) * out) * g1_ref[...] + b1_ref[...]
  x1_ref[0] = x1
  h2f = _layernorm(x1) * (1.0 + sc2_ref[0]) + sh2_ref[0]
  h2t_ref[...] = h2f.T.astype(BF16)
  h2 = h2f.astype(BF16)
  qp = jnp.dot(h2, wq_ref[...], preferred_element_type=F32).astype(BF16)
  half = keys_ref.shape[2]
  for hc in range(keys_ref.shape[0]):
    st_ref[hc] = lax.dot_general(keys_ref[hc], qp[:, hc * half:(hc + 1) * half],
                                 (((1,), (1,)), ((), ())), preferred_element_type=F32)


def _merge_out(attn, sgu, ga, gg, x, gt1, sh2, sc2, ln1_g, ln1_b, w_attn_up, w_gmlp_up, w_out, w_peer_q,
               sub_keys, alpha):
  bsz, s, d = x.shape
  tm = 256
  assert s % tm == 0
  nt = s // tm
  hc, nk, half = sub_keys.shape[0] * sub_keys.shape[1], sub_keys.shape[2], sub_keys.shape[3]
  keys = sub_keys.reshape(hc, nk, half).astype(BF16)
  ws = [w.astype(BF16) for w in (w_attn_up, w_gmlp_up, w_out, w_peer_q)]
  assert w_peer_q.shape[1] == hc * half
  row = lambda width: pl.BlockSpec((1, tm, width), lambda b, i: (b, i, 0))
  vec = pl.BlockSpec((1, 1, d), lambda b, i: (b, 0, 0))
  full = lambda a: pl.BlockSpec(a.shape, lambda b, i: (0,) * a.ndim)
  aw = attn.shape[-1]
  return pl.pallas_call(
      functools.partial(_merge_kernel, alpha=alpha),
      grid=(bsz, nt),
      in_specs=[row(aw), row(sgu.shape[-1]), row(d), row(d), row(d), vec, vec, vec,
                full(ln1_g), full(ln1_b)] + [full(w) for w in ws] + [full(keys)],
      out_specs=[row(d), pl.BlockSpec((d, tm), lambda b, i: (0, b * nt + i)),
                 pl.BlockSpec((hc, nk, tm), lambda b, i: (0, 0, b * nt + i))],
      out_shape=[jax.ShapeDtypeStruct((bsz, s, d), F32), jax.ShapeDtypeStruct((d, bsz * s), BF16),
                 jax.ShapeDtypeStruct((hc, nk, bsz * s), F32)],
      compiler_params=_params("arbitrary", "arbitrary"),
      name="merge_out",
  )(attn, sgu, ga, gg, x, gt1, sh2, sc2, ln1_g, ln1_b, *ws, keys)


def _young_cells():
  return [(a, b) for a in range(PEER_TOPK) for b in range(PEER_TOPK) if (a + 1) * (b + 1) <= PEER_TOPK]


def _route_kernel(st_ref, ca_ref, na_ref, eb_ref, rb_ref, tv_scr, cell_scr, cell2_scr):
  nk, tn = st_ref.shape[1], st_ref.shape[2]
  row = lax.broadcasted_iota(jnp.int32, (nk, tn), 0)
  cells = _young_cells()
  npad = cell_scr.shape[0]
  crow = lax.broadcasted_iota(jnp.int32, (npad, tn), 0)
  kf = float(PEER_TOPK)

  def top_sorted(s, half):
    rank = jnp.full((nk, tn), float(nk), F32)
    for r in range(PEER_TOPK):
      mx = jnp.max(s, axis=0, keepdims=True)
      idx = jnp.min(jnp.where(s == mx, row, nk), axis=0, keepdims=True)
      pick = row == idx
      rank = jnp.where(pick, float(r), rank)
      s = jnp.where(pick, -jnp.inf, s)
      tv_scr[half, r:r + 1, :] = mx
    return rank

  def top_values(s, n):
    vals = []
    for _ in range(n):
      mx = jnp.max(s, axis=0, keepdims=True)
      vals.append(mx)
      s = jnp.where(s == mx, -jnp.inf, s)
    return vals

  def count_ge(s, thr):
    return jnp.sum(jnp.where(s >= thr, 1.0, 0.0), axis=0, keepdims=True)

  def sorted_top(s):
    k = PEER_TOPK
    v = [s[8 * i:8 * (i + 1)] for i in range(k)]

    def order(i, l, descending):
      hi, lo = jnp.maximum(v[i], v[l]), jnp.minimum(v[i], v[l])
      v[i], v[l] = (hi, lo) if descending else (lo, hi)

    def merge(span, direction_bit):
      j = span // 2
      while j >= 1:
        for i in range(k):
          if i ^ j > i:
            order(i, i ^ j, (i & direction_bit) == 0)
        j //= 2

    span = 2
    while span <= k:
      merge(span, span)
      span *= 2
    for shift in (7, 6, 4):
      other = [pltpu.roll(v[k - 1 - i], shift, 0) for i in range(k)]
      for i in range(k):
        v[i] = jnp.maximum(v[i], other[i])
      merge(k, k)
    return [v[i][0:1, :] for i in range(k)]

  def distinct(vals):
    ok = vals[0] > vals[1]
    for r in range(1, len(vals) - 1):
      ok = ok & (vals[r] > vals[r + 1])
    return ok

  def head_no_ties(h, cell_scr):
    n = PEER_TOPK
    s0, s1 = st_ref[2 * h], st_ref[2 * h + 1]
    tv0, tv1 = sorted_top(s0), sorted_top(s1)
    ok = (count_ge(s0, tv0[-1]) == float(n)) & (count_ge(s1, tv1[-1]) == float(n)) & distinct(tv0) & distinct(tv1)
    for ci, (a, b) in enumerate(cells):
      cell_scr[ci:ci + 1, :] = tv0[a] + tv1[b]
    if npad > len(cells):
      cell_scr[len(cells):, :] = jnp.full((npad - len(cells), tn), -jnp.inf, F32)
    cand = cell_scr[...]
    cv = top_values(cand, n)
    ok = ok & (count_ge(cand, cv[-1]) == float(n))
    z = jnp.ones((1, tn), F32)
    for r in range(1, PEER_TOPK):
      z = z + jnp.exp(cv[r] - cv[0])
    cell_scr[...] = jnp.where(cand >= cv[PEER_TOPK - 1], 1.0, 0.0)
    na = jnp.zeros((nk, tn), F32)
    start = 0
    for a in range(PEER_TOPK):
      cnt = sum(1 for (aa, _) in cells if aa == a)
      n_a = jnp.sum(cell_scr[start:start + cnt, :], axis=0, keepdims=True)
      na = jnp.where(s0 == tv0[a], n_a, na)
      start += cnt
    rb = jnp.zeros((nk, tn), F32)
    for r in range(PEER_TOPK):
      rb = rb + jnp.where(tv1[r] > s1, 1.0, 0.0)
    ca_ref[h] = jnp.where(s0 >= tv0[PEER_TOPK - 1], 0.5 * jnp.exp(s0 - tv0[0]) / z, 0.0)
    na_ref[h] = na
    eb_ref[h] = jnp.exp(s1 - tv1[0]).astype(eb_ref.dtype)
    rb_ref[h] = rb.astype(rb_ref.dtype)
    okf = jnp.broadcast_to(jnp.where(ok, 1.0, 0.0), (8, tn))
    return jnp.min(jnp.min(okf, axis=1, keepdims=True), axis=0, keepdims=True)[0, 0] > 0.5

  def head_pair(p, carry):
    no_ties = [head_no_ties(2 * p + q, (cell_scr, cell2_scr)[q]) for q in range(2)]
    for q in range(2):
      @pl.when(jnp.logical_not(no_ties[q]))
      def _():
        head_exact(2 * p + q)

    return carry

  def head_exact(h):
    rank0 = top_sorted(st_ref[2 * h], 0)
    rank1 = top_sorted(st_ref[2 * h + 1], 1)
    tv0 = tv_scr[0]
    tv1 = tv_scr[1]
    for ci, (a, b) in enumerate(cells):
      cell_scr[ci:ci + 1, :] = tv0[a:a + 1] + tv1[b:b + 1]
    if npad > len(cells):
      cell_scr[len(cells):, :] = jnp.full((npad - len(cells), tn), -jnp.inf, F32)
    cand = cell_scr[...]
    top = tv0[0:1] + tv1[0:1]
    picked = jnp.zeros((npad, tn), F32)
    z = jnp.zeros((1, tn), F32)
    for _ in range(PEER_TOPK):
      mx = jnp.max(cand, axis=0, keepdims=True)
      idx = jnp.min(jnp.where(cand == mx, crow, npad), axis=0, keepdims=True)
      pick = crow == idx
      picked = jnp.where(pick, 1.0, picked)
      cand = jnp.where(pick, -jnp.inf, cand)
      z = z + jnp.exp(mx - top)
    cell_scr[...] = picked
    na = jnp.zeros((nk, tn), F32)
    start = 0
    for a in range(PEER_TOPK):
      cnt = sum(1 for (aa, _) in cells if aa == a)
      n_a = jnp.sum(cell_scr[start:start + cnt, :], axis=0, keepdims=True)
      na = jnp.where(rank0 == float(a), n_a, na)
      start += cnt
    ca_ref[h] = jnp.where(rank0 < kf, 0.5 * jnp.exp(st_ref[2 * h] - tv0[0:1]) / z, 0.0)
    na_ref[h] = na
    eb_ref[h] = jnp.exp(st_ref[2 * h + 1] - tv1[0:1]).astype(eb_ref.dtype)
    rb_ref[h] = rank1.astype(rb_ref.dtype)

  lax.fori_loop(0, ca_ref.shape[0] // 2, head_pair, 0)


def _peer_route(st):
  hc, nk, t = st.shape
  heads = hc // 2
  tn = LANES
  assert t % tn == 0
  npad = -(-len(_young_cells()) // 8) * 8
  out = lambda dt: jax.ShapeDtypeStruct((heads, nk, t), dt)
  ospec = pl.BlockSpec((heads, nk, tn), lambda i: (0, 0, i))
  return pl.pallas_call(
      _route_kernel,
      grid=(t // tn,),
      in_specs=[pl.BlockSpec((hc, nk, tn), lambda i: (0, 0, i))],
      out_specs=[ospec] * 4,
      out_shape=[out(F32), out(F32), out(BF16), out(BF16)],
      scratch_shapes=[pltpu.VMEM((2, PEER_TOPK, tn), F32), pltpu.VMEM((npad, tn), F32), pltpu.VMEM((npad, tn), F32)],
      compiler_params=_params("arbitrary"),
      name="peer_route",
  )(st)


PEER_SUB_EXPERTS = 512
PEER_UNIT_PIECES = 1

BF16_ROWS = 16


def _peer_dense_kernel(h2t_ref, u_ref, v_ref, ca_ref, na_ref, eb_ref, rb_ref, x1_ref, gt2_ref, g2_ref, b2_ref,
                       o_ref, acc_scr, at_scr, p_scr, pt_scr, *, alpha):
  e = pl.program_id(1)
  ec = u_ref.shape[0]
  tn = h2t_ref.shape[1]
  heads, nk = eb_ref.shape[0], eb_ref.shape[1]
  sub = PEER_SUB_EXPERTS
  n_sub = ec // sub

  @pl.when(e == 0)
  def _():
    acc_scr[...] = jnp.zeros_like(acc_scr)

  parts = PEER_UNIT_PIECES
  piece = sub // parts
  d_model = v_ref.shape[1]
  qcols = d_model // parts

  def pre_activations(sc, part):
    rows = slice(part * piece, (part + 1) * piece)
    at_scr[sc % 2, rows, :] = jnp.dot(u_ref[sc * sub + part * piece:sc * sub + (part + 1) * piece, :], h2t_ref[...],
                                      preferred_element_type=F32)

  def packed_rows(ref, h, r, ls):
    return jnp.broadcast_to(ref[h, r:r + 1, ls], (BF16_ROWS, LANES)).astype(BF16)

  zero = jnp.zeros((), BF16)

  def gate(ca, na, rb, eb):
    return ca * jnp.minimum(jnp.maximum(na - rb, zero), eb)

  def gated_activations(sc, part):
    slot = sc % 2
    for ii in range(piece // nk):
      i_local = part * (piece // nk) + ii
      r = sc * (sub // nk) + i_local
      for lc in range(tn // LANES):
        ls = slice(lc * LANES, (lc + 1) * LANES)
        ca = [packed_rows(ca_ref, h, r, ls) for h in range(heads)]
        na = [packed_rows(na_ref, h, r, ls) for h in range(heads)]
        for rc in range(nk // BF16_ROWS):
          js = slice(rc * BF16_ROWS, (rc + 1) * BF16_ROWS)
          w = gate(ca[0], na[0], rb_ref[0, js, ls], eb_ref[0, js, ls])
          for h in range(1, heads):
            w = w + gate(ca[h], na[h], rb_ref[h, js, ls], eb_ref[h, js, ls])
          rows = slice(i_local * nk + rc * BF16_ROWS, i_local * nk + (rc + 1) * BF16_ROWS)
          a = at_scr[slot, rows, ls]
          p_scr[slot, rows, ls] = w * (a + a * lax.erf(a * (2.0 ** -0.5))).astype(BF16)

  def transpose_gated(sc):
    pt_scr[sc % 2] = p_scr[sc % 2].T

  def values(sc, q):
    cols = slice(q * qcols, (q + 1) * qcols)
    acc_scr[:, cols] += jnp.dot(pt_scr[sc % 2], v_ref[sc * sub:(sc + 1) * sub, cols], preferred_element_type=F32)

  for part in range(parts):
    pre_activations(0, part)
  for sc in range(n_sub):
    for part in range(parts):
      if sc + 1 < n_sub:
        pre_activations(sc + 1, part)
      if sc >= 1:
        values(sc - 1, part)
      gated_activations(sc, part)
    transpose_gated(sc)
  for q in range(parts):
    values(n_sub - 1, q)

  @pl.when(e == pl.num_programs(1) - 1)
  def _():
    y = alpha * x1_ref[...] + (1.0 + gt2_ref[0]) * acc_scr[...]
    o_ref[...] = _layernorm(y) * g2_ref[...] + b2_ref[...]


def _peer_dense(h2t, u, v, ca, na, eb, rb, x1, gt2, ln2_g, ln2_b, seq, alpha):
  d, t = h2t.shape
  n_exp = u.shape[0]
  heads, nk, _ = ca.shape
  assert n_exp == nk * nk
  per_step = 16
  tn, ec = 512, per_step * nk
  assert seq % tn == 0 and n_exp % ec == 0 and ec % PEER_SUB_EXPERTS == 0 and PEER_SUB_EXPERTS % nk == 0
  ub, vb = u.astype(BF16), v.astype(BF16)
  tok = lambda width: pl.BlockSpec((tn, width), lambda i, e: (i, 0))
  expert = pl.BlockSpec((ec, d), lambda i, e: (e, 0))
  fac = pl.BlockSpec((heads, nk, tn), lambda i, e: (0, 0, i))
  fac_i = pl.BlockSpec((heads, per_step, tn), lambda i, e: (0, e, i))
  one = pl.BlockSpec((1, d), lambda i, e: (0, 0))
  return pl.pallas_call(
      functools.partial(_peer_dense_kernel, alpha=alpha),
      grid=(t // tn, n_exp // ec),
      in_specs=[pl.BlockSpec((d, tn), lambda i, e: (0, i)), expert, expert, fac_i, fac_i, fac, fac, tok(d),
                pl.BlockSpec((1, 1, d), lambda i, e: ((i * tn) // seq, 0, 0)), one, one],
      out_specs=tok(d),
      out_shape=jax.ShapeDtypeStruct((t, d), F32),
      scratch_shapes=[pltpu.VMEM((tn, d), F32), pltpu.VMEM((2, PEER_SUB_EXPERTS, tn), F32),
                      pltpu.VMEM((2, PEER_SUB_EXPERTS, tn), BF16), pltpu.VMEM((2, tn, PEER_SUB_EXPERTS), BF16)],
      compiler_params=_params("arbitrary", "arbitrary"),
      name="peer_dense",
  )(h2t, ub, vb, ca, na, eb, rb, x1, gt2, ln2_g, ln2_b)


def kernel(x, c, w_cond, b_cond, w_in, w_attn_up, w_gmlp_up, w_out, sgu_ln_g, sgu_ln_b, w_spatial, b_spatial,
           ln1_g, ln1_b, w_peer_q, peer_sub_keys, peer_u, peer_v, ln2_g, ln2_b):
  bsz, s, d = x.shape
  t = bsz * s
  depth = w_cond.shape[0]
  alpha = float((2.0 * depth) ** 0.25)
  for l in range(depth):
    mod = _cond(c, w_cond[l], b_cond[l])
    sh1, sc1, gt1, sh2, sc2, gt2 = [m[:, None, :] for m in jnp.split(mod, 6, axis=-1)]
    q, k, v, kmean, gu, gv, ga, gg = _in_proj(x, sh1, sc1, w_in[l])
    attn = _moba_attention(q, k, v, kmean)
    sgu = _sgu(gu.reshape(t, -1), gv.reshape(t, -1), sgu_ln_g[l], sgu_ln_b[l], w_spatial[l], b_spatial[l])
    x1, h2t, st = _merge_out(attn, sgu.reshape(bsz, s, -1), ga, gg, x, gt1, sh2, sc2,
                             ln1_g[l].reshape(1, d), ln1_b[l].reshape(1, d),
                             w_attn_up[l], w_gmlp_up[l], w_out[l], w_peer_q[l], peer_sub_keys[l], alpha)
    ca, na, eb, rb = _peer_route(st)
    x = _peer_dense(h2t, peer_u[l], peer_v[l], ca, na, eb, rb, x1.reshape(t, d), gt2,
                    ln2_g[l].reshape(1, d), ln2_b[l].reshape(1, d), s, alpha).reshape(bsz, s, d)
  return x
```

```python
import functools

import jax
import jax.numpy as jnp
from jax import lax
from jax.experimental import pallas as pl
from jax.experimental.pallas import tpu as pltpu

F32 = jnp.float32
BF16 = jnp.bfloat16
HIGHEST = lax.Precision.HIGHEST

N_ATTN_HEADS = 8
ATTN_HEAD_DIM = 64
MOBA_BLOCK = 256
MOBA_TOPK = 3
GMLP_GROUPS = 8
GMLP_GROUP_DIM = 64
GMLP_CHUNK = 128
PEER_HEADS = 8
PEER_N_KEYS = 128
PEER_TOPK = 16
LN_EPS = 1e-5

LANES = 128
V7X_VMEM_LIMIT_BYTES = 56 * 1024 * 1024

NEG_BIG = -1e30
LOG2E = 1.4426950408889634


def _params(*sem):
  return pltpu.CompilerParams(dimension_semantics=sem, vmem_limit_bytes=V7X_VMEM_LIMIT_BYTES)


def _layernorm(x):
  mu = jnp.mean(x, axis=-1, keepdims=True)
  d = x - mu
  var = jnp.mean(d * d, axis=-1, keepdims=True)
  return d * lax.rsqrt(var + LN_EPS)


def _gelu(x):
  return 0.5 * x * (1.0 + lax.erf(x * (2.0 ** -0.5)))


def _cond_kernel(c_ref, w_ref, b_ref, o_ref):
  cs = jax.nn.silu(c_ref[...])
  o_ref[...] = jnp.dot(cs, w_ref[...], preferred_element_type=F32, precision=HIGHEST) + b_ref[...]


def _cond(c, w, b):
  bsz, d = c.shape
  n = w.shape[1]
  rows = 8
  cp = jnp.zeros((rows, d), F32).at[:bsz].set(c)
  tn = 1536 if n % 1536 == 0 else n
  out = pl.pallas_call(
      _cond_kernel,
      grid=(n // tn,),
      in_specs=[
          pl.BlockSpec((rows, d), lambda j: (0, 0)),
          pl.BlockSpec((d, tn), lambda j: (0, j)),
          pl.BlockSpec((1, tn), lambda j: (0, j)),
      ],
      out_specs=pl.BlockSpec((rows, tn), lambda j: (0, j)),
      out_shape=jax.ShapeDtypeStruct((rows, n), F32),
      compiler_params=_params("arbitrary"),
      name="adaln_cond",
  )(cp, w, b.reshape(1, n))
  return out[:bsz]


def _inproj_kernel(x_ref, sh_ref, sc_ref, wq, wk, wv, wgu, wgv, wga, wgg,
                   q_ref, k_ref, v_ref, km_ref, gu_ref, gv_ref, ga_ref, gg_ref):
  h = (_layernorm(x_ref[0]) * (1.0 + sc_ref[0]) + sh_ref[0]).astype(BF16)
  dot = lambda w: jnp.dot(h, w[...], preferred_element_type=F32)
  q_ref[0] = dot(wq).T
  k = dot(wk)
  tm, width = k.shape
  km_ref[0, 0] = jnp.mean(k.reshape(tm // MOBA_BLOCK, MOBA_BLOCK, width), axis=1)
  k_ref[0] = k.astype(BF16)
  v = dot(wv)
  for kb in range(tm // MOBA_BLOCK):
    v_ref[0, kb] = v[kb * MOBA_BLOCK:(kb + 1) * MOBA_BLOCK].T.astype(BF16)
  gu_ref[0] = dot(wgu)
  gv_ref[0] = dot(wgv)
  ga_ref[0] = dot(wga)
  gg_ref[0] = dot(wgg)


def _in_proj(x, sh, sc, w_in):
  bsz, s, d = x.shape
  aw = N_ATTN_HEADS * ATTN_HEAD_DIM
  gw = GMLP_GROUPS * GMLP_GROUP_DIM
  bounds = [0, aw, 2 * aw, 3 * aw, 3 * aw + gw, 3 * aw + 2 * gw, 3 * aw + 2 * gw + d, 3 * aw + 2 * gw + 2 * d]
  assert w_in.shape[1] == bounds[-1]
  ws = [w_in[:, a:b].astype(BF16) for a, b in zip(bounds[:-1], bounds[1:])]
  tm = 512
  assert s % tm == 0 and tm % MOBA_BLOCK == 0
  nt = s // tm
  row = lambda width: pl.BlockSpec((1, tm, width), lambda b, i: (b, i, 0))
  vec = pl.BlockSpec((1, 1, d), lambda b, i: (b, 0, 0))
  wspec = lambda w: pl.BlockSpec(w.shape, lambda b, i: (0, 0))
  sds = lambda width, dt: jax.ShapeDtypeStruct((bsz, s, width), dt)
  kpt = tm // MOBA_BLOCK
  outs = pl.pallas_call(
      _inproj_kernel,
      grid=(bsz, nt),
      in_specs=[row(d), vec, vec] + [wspec(w) for w in ws],
      out_specs=[pl.BlockSpec((1, aw, tm), lambda b, i: (b, 0, i)), row(aw),
                 pl.BlockSpec((1, kpt, aw, MOBA_BLOCK), lambda b, i: (b, i, 0, 0)),
                 pl.BlockSpec((1, 1, kpt, aw), lambda b, i: (b, i, 0, 0)),
                 row(gw), row(gw), row(d), row(d)],
      out_shape=[jax.ShapeDtypeStruct((bsz, aw, s), F32), sds(aw, BF16),
                 jax.ShapeDtypeStruct((bsz, s // MOBA_BLOCK, aw, MOBA_BLOCK), BF16),
                 jax.ShapeDtypeStruct((bsz, nt, kpt, aw), F32),
                 sds(gw, F32), sds(gw, F32), sds(d, F32), sds(d, F32)],
      compiler_params=_params("arbitrary", "arbitrary"),
      name="in_proj",
  )(x, sh, sc, *ws)
  q, k, v, km, gu, gv, ga, gg = outs
  return q, k, v, km.reshape(bsz, s // MOBA_BLOCK, aw), gu, gv, ga, gg


def _alibi_slopes():
  return [2.0 ** (-8.0 * (i + 1) / N_ATTN_HEADS) for i in range(N_ATTN_HEADS)]


def _key_side_table(s):
  nb = s // MOBA_BLOCK
  assert nb + 6 <= LANES
  pos = jnp.arange(s, dtype=F32)
  blk = jnp.arange(s, dtype=jnp.int32) // MOBA_BLOCK
  onehot = (blk[:, None] == jnp.arange(nb)[None, :]).astype(F32)
  slopes = _alibi_slopes()

  def top16(a):
    bits = lax.bitcast_convert_type(a, jnp.uint32) & jnp.uint32(0xFFFF0000)
    return lax.bitcast_convert_type(bits, F32)

  tables = []
  for hp in range(N_ATTN_HEADS // 2):
    cols = [onehot]
    for h in (2 * hp, 2 * hp + 1):
      c = (slopes[h] * LOG2E) * pos
      hi = top16(c)
      mid = top16(c - hi)
      lo = c - hi - mid
      cols += [hi[:, None], mid[:, None], lo[:, None]]
    t = jnp.concatenate(cols, axis=1)
    tables.append(jnp.pad(t, ((0, 0), (0, LANES - t.shape[1]))))
  return jnp.stack(tables).astype(BF16)


ATTN_BLOCKS_PER_STEP = 2


def _attn_kernel(qt_ref, k_ref, kb_ref, vt_ref, km_ref, o_ref, m_scr, l_scr, acc_scr, sa_scr, sb_scr, *, nb):
  j = pl.program_id(2)
  tq = MOBA_BLOCK
  hd = ATTN_HEAD_DIM
  q2 = qt_ref[0]
  row = lax.broadcasted_iota(jnp.int32, (LANES, tq), 0)
  qscale = (hd ** -0.5) * LOG2E

  rhs_own, rhs_past = [], []
  for h in range(2):
    qh = jnp.where((row >= hd * h) & (row < hd * (h + 1)), q2, 0.0)
    gate = jnp.dot(km_ref[0], qh, preferred_element_type=F32, precision=HIGHEST)
    blk = lax.broadcasted_iota(jnp.int32, (nb, tq), 0)
    g = jnp.where(blk < j, gate, -jnp.inf)
    sel = blk < 0
    for _ in range(MOBA_TOPK):
      mx = jnp.max(g, axis=0, keepdims=True)
      idx = jnp.min(jnp.where(g == mx, blk, nb), axis=0, keepdims=True)
      pick = (blk == idx) & (g > -jnp.inf)
      sel = sel | pick
      g = jnp.where(pick, -jnp.inf, g)
    rest = lax.broadcasted_iota(jnp.int32, (LANES - nb, tq), 0)
    tail = jnp.where((rest >= 3 * h) & (rest < 3 * h + 3), 1.0, 0.0)
    qb = (qh * qscale).astype(BF16)
    side_past = jnp.concatenate([jnp.where(sel, 0.0, NEG_BIG), tail], axis=0).astype(BF16)
    side_own = jnp.concatenate([jnp.where(blk == j, 0.0, NEG_BIG), tail], axis=0).astype(BF16)
    rhs_past.append(jnp.concatenate([qb, side_past], axis=0))
    rhs_own.append(jnp.concatenate([qb, side_own], axis=0))

  def keys(off, n):
    return jnp.concatenate([k_ref[0, pl.ds(off, n), :], kb_ref[0, pl.ds(off, n), :]], axis=1)

  own = pl.multiple_of(j * tq, tq)
  kr = lax.broadcasted_iota(jnp.int32, (tq, tq), 0)
  qc = lax.broadcasted_iota(jnp.int32, (tq, tq), 1)
  k_own = keys(own, tq)
  for h in range(2):
    s = jnp.where(kr <= qc, jnp.dot(k_own, rhs_own[h], preferred_element_type=F32), -jnp.inf)
    m = jnp.max(s, axis=0, keepdims=True)
    p = jnp.exp2(s - m)
    m_scr[h] = m
    l_scr[h] = jnp.sum(p, axis=0, keepdims=True)
    acc_scr[h] = jnp.dot(vt_ref[0, j, hd * h:hd * (h + 1), :], p.astype(BF16), preferred_element_type=F32)

  per = ATTN_BLOCKS_PER_STEP
  n_chunks = (j + per - 1) // per
  c_last = nb // per - 1

  def qk(c, s_ref):
    kk = keys(pl.multiple_of(c * (per * tq), per * tq), per * tq)
    for h in range(2):
      s_ref[h] = jnp.dot(kk, rhs_past[h], preferred_element_type=F32)

  def softmax_pv(c, s_ref):
    for h in range(2):
      s = s_ref[h]
      m_old = m_scr[h]
      m_new = jnp.maximum(m_old, jnp.max(s, axis=0, keepdims=True))
      alpha = jnp.exp2(m_old - m_new)
      p = jnp.exp2(s - m_new)
      m_scr[h] = m_new
      l_scr[h] = alpha * l_scr[h] + jnp.sum(p, axis=0, keepdims=True)
      p = p.astype(BF16)
      pv = sum(jnp.dot(vt_ref[0, c * per + i, hd * h:hd * (h + 1), :], p[i * tq:(i + 1) * tq],
                       preferred_element_type=F32) for i in range(per))
      acc_scr[h] = alpha * acc_scr[h] + pv

  @pl.when(n_chunks > 0)
  def _():
    qk(0, sa_scr)

  def pair(i, carry):
    qk(2 * i + 1, sb_scr)
    softmax_pv(2 * i, sa_scr)
    qk(jnp.minimum(2 * i + 2, c_last), sa_scr)
    softmax_pv(2 * i + 1, sb_scr)
    return carry

  lax.fori_loop(0, (n_chunks + 1) // 2, pair, 0)
  out_t = jnp.concatenate([acc_scr[0] / l_scr[0], acc_scr[1] / l_scr[1]], axis=0)
  o_ref[0] = out_t.T.astype(o_ref.dtype)


def _moba_attention(qt, k, vt, kmean):
  bsz, aw, s = qt.shape
  nb = s // MOBA_BLOCK
  per = ATTN_BLOCKS_PER_STEP
  assert s % MOBA_BLOCK == 0 and 2 * ATTN_HEAD_DIM == LANES and nb % (2 * per) == 0
  kb = _key_side_table(s)
  tq = MOBA_BLOCK
  return pl.pallas_call(
      functools.partial(_attn_kernel, nb=nb),
      grid=(bsz, N_ATTN_HEADS // 2, nb),
      in_specs=[
          pl.BlockSpec((1, LANES, tq), lambda b, hp, j: (b, hp, j)),
          pl.BlockSpec((1, s, LANES), lambda b, hp, j: (b, 0, hp)),
          pl.BlockSpec((1, s, LANES), lambda b, hp, j: (hp, 0, 0)),
          pl.BlockSpec((1, nb, LANES, tq), lambda b, hp, j: (b, 0, hp, 0)),
          pl.BlockSpec((1, nb, LANES), lambda b, hp, j: (b, 0, hp)),
      ],
      out_specs=pl.BlockSpec((1, tq, LANES), lambda b, hp, j: (b, j, hp)),
      out_shape=jax.ShapeDtypeStruct((bsz, s, aw), BF16),
      scratch_shapes=[pltpu.VMEM((2, 1, tq), F32), pltpu.VMEM((2, 1, tq), F32),
                      pltpu.VMEM((2, ATTN_HEAD_DIM, tq), F32),
                      pltpu.VMEM((2, per * tq, tq), F32), pltpu.VMEM((2, per * tq, tq), F32)],
      compiler_params=_params("arbitrary", "arbitrary", "arbitrary"),
      name="moba_attn",
  )(qt, k, kb, vt, kmean)


def _split2(x):
  hi = x.astype(BF16)
  lo = (x - hi.astype(F32)).astype(BF16)
  return hi, lo


def _sgu_kernel(gu_ref, gv_ref, avg_ref, lng_ref, lnb_ref, wsp_ref, bsp_ref, o_ref):
  tm, width = gu_ref.shape
  avg = avg_ref[...]

  def group_mean(a):
    return sum(jnp.dot(p, avg, preferred_element_type=F32) for p in _split2(a))

  v = _gelu(gv_ref[...])
  d = v - group_mean(v)
  vv = (d * lax.rsqrt(group_mean(d * d) + LN_EPS) * lng_ref[...] + lnb_ref[...]).astype(BF16)
  u = _gelu(gu_ref[...])

  rows = lax.broadcasted_iota(jnp.int32, (GMLP_CHUNK, GMLP_CHUNK), 0)
  cols = lax.broadcasted_iota(jnp.int32, (GMLP_CHUNK, GMLP_CHUNK), 1)
  grp = lax.broadcasted_iota(jnp.int32, (GMLP_CHUNK, width), 1) // GMLP_GROUP_DIM
  ws = [jnp.where(rows >= cols, wsp_ref[g], 0.0).astype(BF16) for g in range(GMLP_GROUPS)]
  for ck in range(tm // GMLP_CHUNK):
    sl = slice(ck * GMLP_CHUNK, (ck + 1) * GMLP_CHUNK)
    vc = vv[sl]
    sv = bsp_ref[...]
    for g in range(GMLP_GROUPS):
      sv = sv + jnp.where(grp == g, jnp.dot(ws[g], vc, preferred_element_type=F32), 0.0)
    o_ref[sl, :] = (u[sl] * sv).astype(o_ref.dtype)


def _sgu(gu, gv, ln_g, ln_b, w_spatial, b_spatial):
  t, width = gu.shape
  tm = 512
  assert t % tm == 0 and tm % GMLP_CHUNK == 0 and width == GMLP_GROUPS * GMLP_GROUP_DIM
  ch = jnp.arange(width) // GMLP_GROUP_DIM
  avg = ((ch[:, None] == ch[None, :]).astype(F32) / GMLP_GROUP_DIM).astype(BF16)
  bsp = jnp.repeat(b_spatial.T, GMLP_GROUP_DIM, axis=1)
  row = pl.BlockSpec((tm, width), lambda i: (i, 0))
  full = lambda a: pl.BlockSpec(a.shape, lambda i: (0,) * a.ndim)
  args = (gu, gv, avg, ln_g.reshape(1, width), ln_b.reshape(1, width), w_spatial, bsp)
  return pl.pallas_call(
      _sgu_kernel,
      grid=(t // tm,),
      in_specs=[row, row] + [full(a) for a in args[2:]],
      out_specs=row,
      out_shape=jax.ShapeDtypeStruct((t, width), BF16),
      compiler_params=_params("arbitrary"),
      name="sgu",
  )(*args)


def _merge_kernel(attn_ref, sgu_ref, ga_ref, gg_ref, x_ref, gt1_ref, sh2_ref, sc2_ref, g1_ref, b1_ref,
                  wa_ref, wg_ref, wo_ref, wq_ref, keys_ref, x1_ref, h2t_ref, st_ref, *, alpha):
  a = jnp.dot(attn_ref[0], wa_ref[...], preferred_element_type=F32)
  g = jnp.dot(sgu_ref[0], wg_ref[...], preferred_element_type=F32)
  merged = jax.nn.sigmoid(ga_ref[0]) * a + jax.nn.sigmoid(gg_ref[0]) * g
  out = jnp.dot(merged.astype(BF16), wo_ref[...], preferred_element_type=F32)
  x1 = _layernorm(alpha * x_ref[0] + (1.0 + gt1_ref[0]) * out) * g1_ref[...] + b1_ref[...]
  x1_ref[0] = x1
  h2f = _layernorm(x1) * (1.0 + sc2_ref[0]) + sh2_ref[0]
  h2t_ref[...] = h2f.T.astype(BF16)
  h2 = h2f.astype(BF16)
  qp = jnp.dot(h2, wq_ref[...], preferred_element_type=F32).astype(BF16)
  half = keys_ref.shape[2]
  for hc in range(keys_ref.shape[0]):
    st_ref[hc] = lax.dot_general(keys_ref[hc], qp[:, hc * half:(hc + 1) * half],
                                 (((1,), (1,)), ((), ())), preferred_element_type=F32)


def _merge_out(attn, sgu, ga, gg, x, gt1, sh2, sc2, ln1_g, ln1_b, w_attn_up, w_gmlp_up, w_out, w_peer_q,
               sub_keys, alpha):
  bsz, s, d = x.shape
  tm = 256
  assert s % tm == 0
  nt = s // tm
  hc, nk, half = sub_keys.shape[0] * sub_keys.shape[1], sub_keys.shape[2], sub_keys.shape[3]
  keys = sub_keys.reshape(hc, nk, half).astype(BF16)
  ws = [w.astype(BF16) for w in (w_attn_up, w_gmlp_up, w_out, w_peer_q)]
  assert w_peer_q.shape[1] == hc * half
  row = lambda width: pl.BlockSpec((1, tm, width), lambda b, i: (b, i, 0))
  vec = pl.BlockSpec((1, 1, d), lambda b, i: (b, 0, 0))
  full = lambda a: pl.BlockSpec(a.shape, lambda b, i: (0,) * a.ndim)
  aw = attn.shape[-1]
  return pl.pallas_call(
      functools.partial(_merge_kernel, alpha=alpha),
      grid=(bsz, nt),
      in_specs=[row(aw), row(sgu.shape[-1]), row(d), row(d), row(d), vec, vec, vec,
                full(ln1_g), full(ln1_b)] + [full(w) for w in ws] + [full(keys)],
      out_specs=[row(d), pl.BlockSpec((d, tm), lambda b, i: (0, b * nt + i)),
                 pl.BlockSpec((hc, nk, tm), lambda b, i: (0, 0, b * nt + i))],
      out_shape=[jax.ShapeDtypeStruct((bsz, s, d), F32), jax.ShapeDtypeStruct((d, bsz * s), BF16),
                 jax.ShapeDtypeStruct((hc, nk, bsz * s), F32)],
      compiler_params=_params("arbitrary", "arbitrary"),
      name="merge_out",
  )(attn, sgu, ga, gg, x, gt1, sh2, sc2, ln1_g, ln1_b, *ws, keys)


def _young_cells():
  return [(a, b) for a in range(PEER_TOPK) for b in range(PEER_TOPK) if (a + 1) * (b + 1) <= PEER_TOPK]


def _route_kernel(st_ref, ca_ref, na_ref, eb_ref, rb_ref, tv_scr, cell_scr, cell2_scr):
  nk, tn = st_ref.shape[1], st_ref.shape[2]
  row = lax.broadcasted_iota(jnp.int32, (nk, tn), 0)
  cells = _young_cells()
  npad = cell_scr.shape[0]
  crow = lax.broadcasted_iota(jnp.int32, (npad, tn), 0)
  kf = float(PEER_TOPK)

  def top_sorted(s, half):
    rank = jnp.full((nk, tn), float(nk), F32)
    for r in range(PEER_TOPK):
      mx = jnp.max(s, axis=0, keepdims=True)
      idx = jnp.min(jnp.where(s == mx, row, nk), axis=0, keepdims=True)
      pick = row == idx
      rank = jnp.where(pick, float(r), rank)
      s = jnp.where(pick, -jnp.inf, s)
      tv_scr[half, r:r + 1, :] = mx
    return rank

  def top_values(s, n):
    vals = []
    for _ in range(n):
      mx = jnp.max(s, axis=0, keepdims=True)
      vals.append(mx)
      s = jnp.where(s == mx, -jnp.inf, s)
    return vals

  def count_ge(s, thr):
    return jnp.sum(jnp.where(s >= thr, 1.0, 0.0), axis=0, keepdims=True)

  def sorted_top(s):
    k = PEER_TOPK
    v = [s[8 * i:8 * (i + 1)] for i in range(k)]

    def order(i, l, descending):
      hi, lo = jnp.maximum(v[i], v[l]), jnp.minimum(v[i], v[l])
      v[i], v[l] = (hi, lo) if descending else (lo, hi)

    def merge(span, direction_bit):
      j = span // 2
      while j >= 1:
        for i in range(k):
          if i ^ j > i:
            order(i, i ^ j, (i & direction_bit) == 0)
        j //= 2

    span = 2
    while span <= k:
      merge(span, span)
      span *= 2
    for shift in (7, 6, 4):
      other = [pltpu.roll(v[k - 1 - i], shift, 0) for i in range(k)]
      for i in range(k):
        v[i] = jnp.maximum(v[i], other[i])
      merge(k, k)
    return [v[i][0:1, :] for i in range(k)]

  def distinct(vals):
    ok = vals[0] > vals[1]
    for r in range(1, len(vals) - 1):
      ok = ok & (vals[r] > vals[r + 1])
    return ok

  def head_no_ties(h, cell_scr):
    n = PEER_TOPK
    s0, s1 = st_ref[2 * h], st_ref[2 * h + 1]
    tv0, tv1 = sorted_top(s0), sorted_top(s1)
    ok = (count_ge(s0, tv0[-1]) == float(n)) & (count_ge(s1, tv1[-1]) == float(n)) & distinct(tv0) & distinct(tv1)
    for ci, (a, b) in enumerate(cells):
      cell_scr[ci:ci + 1, :] = tv0[a] + tv1[b]
    if npad > len(cells):
      cell_scr[len(cells):, :] = jnp.full((npad - len(cells), tn), -jnp.inf, F32)
    cand = cell_scr[...]
    cv = top_values(cand, n)
    ok = ok & (count_ge(cand, cv[-1]) == float(n))
    z = jnp.ones((1, tn), F32)
    for r in range(1, PEER_TOPK):
      z = z + jnp.exp(cv[r] - cv[0])
    cell_scr[...] = jnp.where(cand >= cv[PEER_TOPK - 1], 1.0, 0.0)
    na = jnp.zeros((nk, tn), F32)
    start = 0
    for a in range(PEER_TOPK):
      cnt = sum(1 for (aa, _) in cells if aa == a)
      n_a = jnp.sum(cell_scr[start:start + cnt, :], axis=0, keepdims=True)
      na = jnp.where(s0 == tv0[a], n_a, na)
      start += cnt
    assert PEER_TOPK == 16
    c8 = tv1[7] > s1
    c4 = jnp.where(c8, tv1[11], tv1[3]) > s1
    c2 = jnp.where(c8, jnp.where(c4, tv1[13], tv1[9]), jnp.where(c4, tv1[5], tv1[1])) > s1
    c1 = jnp.where(c8, jnp.where(c4, jnp.where(c2, tv1[14], tv1[12]), jnp.where(c2, tv1[10], tv1[8])),
                   jnp.where(c4, jnp.where(c2, tv1[6], tv1[4]), jnp.where(c2, tv1[2], tv1[0]))) > s1
    rb = (jnp.where(c8, 8.0, 0.0) + jnp.where(c4, 4.0, 0.0) + jnp.where(c2, 2.0, 0.0) + jnp.where(c1, 1.0, 0.0)
          + jnp.where(tv1[15] > s1, 1.0, 0.0))
    ca_ref[h] = jnp.where(s0 >= tv0[PEER_TOPK - 1], 0.5 * jnp.exp(s0 - tv0[0]) / z, 0.0)
    na_ref[h] = na
    eb_ref[h] = jnp.exp(s1 - tv1[0]).astype(eb_ref.dtype)
    rb_ref[h] = rb.astype(rb_ref.dtype)
    okf = jnp.broadcast_to(jnp.where(ok, 1.0, 0.0), (8, tn))
    return jnp.min(jnp.min(okf, axis=1, keepdims=True), axis=0, keepdims=True)[0, 0] > 0.5

  def head_pair(p, carry):
    no_ties = [head_no_ties(2 * p + q, (cell_scr, cell2_scr)[q]) for q in range(2)]
    for q in range(2):
      @pl.when(jnp.logical_not(no_ties[q]))
      def _():
        head_exact(2 * p + q)

    return carry

  def head_exact(h):
    rank0 = top_sorted(st_ref[2 * h], 0)
    rank1 = top_sorted(st_ref[2 * h + 1], 1)
    tv0 = tv_scr[0]
    tv1 = tv_scr[1]
    for ci, (a, b) in enumerate(cells):
      cell_scr[ci:ci + 1, :] = tv0[a:a + 1] + tv1[b:b + 1]
    if npad > len(cells):
      cell_scr[len(cells):, :] = jnp.full((npad - len(cells), tn), -jnp.inf, F32)
    cand = cell_scr[...]
    top = tv0[0:1] + tv1[0:1]
    picked = jnp.zeros((npad, tn), F32)
    z = jnp.zeros((1, tn), F32)
    for _ in range(PEER_TOPK):
      mx = jnp.max(cand, axis=0, keepdims=True)
      idx = jnp.min(jnp.where(cand == mx, crow, npad), axis=0, keepdims=True)
      pick = crow == idx
      picked = jnp.where(pick, 1.0, picked)
      cand = jnp.where(pick, -jnp.inf, cand)
      z = z + jnp.exp(mx - top)
    cell_scr[...] = picked
    na = jnp.zeros((nk, tn), F32)
    start = 0
    for a in range(PEER_TOPK):
      cnt = sum(1 for (aa, _) in cells if aa == a)
      n_a = jnp.sum(cell_scr[start:start + cnt, :], axis=0, keepdims=True)
      na = jnp.where(rank0 == float(a), n_a, na)
      start += cnt
    ca_ref[h] = jnp.where(rank0 < kf, 0.5 * jnp.exp(st_ref[2 * h] - tv0[0:1]) / z, 0.0)
    na_ref[h] = na
    eb_ref[h] = jnp.exp(st_ref[2 * h + 1] - tv1[0:1]).astype(eb_ref.dtype)
    rb_ref[h] = rank1.astype(rb_ref.dtype)

  lax.fori_loop(0, ca_ref.shape[0] // 2, head_pair, 0)


def _peer_route(st):
  hc, nk, t = st.shape
  heads = hc // 2
  tn = LANES
  assert t % tn == 0
  npad = -(-len(_young_cells()) // 8) * 8
  out = lambda dt: jax.ShapeDtypeStruct((heads, nk, t), dt)
  ospec = pl.BlockSpec((heads, nk, tn), lambda i: (0, 0, i))
  return pl.pallas_call(
      _route_kernel,
      grid=(t // tn,),
      in_specs=[pl.BlockSpec((hc, nk, tn), lambda i: (0, 0, i))],
      out_specs=[ospec] * 4,
      out_shape=[out(F32), out(F32), out(BF16), out(BF16)],
      scratch_shapes=[pltpu.VMEM((2, PEER_TOPK, tn), F32), pltpu.VMEM((npad, tn), F32), pltpu.VMEM((npad, tn), F32)],
      compiler_params=_params("arbitrary"),
      name="peer_route",
  )(st)


PEER_SUB_EXPERTS = 512
PEER_UNIT_PIECES = 1

BF16_ROWS = 16


def _peer_dense_kernel(h2t_ref, u_ref, v_ref, ca_ref, na_ref, eb_ref, rb_ref, x1_ref, gt2_ref, g2_ref, b2_ref,
                       o_ref, acc_scr, at_scr, p_scr, pt_scr, *, alpha):
  e = pl.program_id(1)
  ec = u_ref.shape[0]
  tn = h2t_ref.shape[1]
  heads, nk = eb_ref.shape[0], eb_ref.shape[1]
  sub = PEER_SUB_EXPERTS
  n_sub = ec // sub

  @pl.when(e == 0)
  def _():
    acc_scr[...] = jnp.zeros_like(acc_scr)

  parts = PEER_UNIT_PIECES
  piece = sub // parts
  d_model = v_ref.shape[1]
  qcols = d_model // parts

  def pre_activations(sc, part):
    rows = slice(part * piece, (part + 1) * piece)
    at_scr[sc % 2, rows, :] = jnp.dot(u_ref[sc * sub + part * piece:sc * sub + (part + 1) * piece, :], h2t_ref[...],
                                      preferred_element_type=F32)

  def packed_rows(ref, h, r, ls):
    return jnp.broadcast_to(ref[h, r:r + 1, ls], (BF16_ROWS, LANES)).astype(BF16)

  zero = jnp.zeros((), BF16)

  def gate(ca, na, rb, eb):
    return ca * jnp.minimum(jnp.maximum(na - rb, zero), eb)

  def gated_activations(sc, part):
    slot = sc % 2
    for ii in range(piece // nk):
      i_local = part * (piece // nk) + ii
      r = sc * (sub // nk) + i_local
      for lc in range(tn // LANES):
        ls = slice(lc * LANES, (lc + 1) * LANES)
        ca = [packed_rows(ca_ref, h, r, ls) for h in range(heads)]
        na = [packed_rows(na_ref, h, r, ls) for h in range(heads)]
        for rc in range(nk // BF16_ROWS):
          js = slice(rc * BF16_ROWS, (rc + 1) * BF16_ROWS)
          w = gate(ca[0], na[0], rb_ref[0, js, ls], eb_ref[0, js, ls])
          for h in range(1, heads):
            w = w + gate(ca[h], na[h], rb_ref[h, js, ls], eb_ref[h, js, ls])
          rows = slice(i_local * nk + rc * BF16_ROWS, i_local * nk + (rc + 1) * BF16_ROWS)
          a = at_scr[slot, rows, ls]
          p_scr[slot, rows, ls] = w * (a + a * lax.erf(a * (2.0 ** -0.5))).astype(BF16)

  def transpose_gated(sc):
    pt_scr[sc % 2] = p_scr[sc % 2].T

  def values(sc, q):
    cols = slice(q * qcols, (q + 1) * qcols)
    acc_scr[:, cols] += jnp.dot(pt_scr[sc % 2], v_ref[sc * sub:(sc + 1) * sub, cols], preferred_element_type=F32)

  for part in range(parts):
    pre_activations(0, part)
  for sc in range(n_sub):
    for part in range(parts):
      if sc + 1 < n_sub:
        pre_activations(sc + 1, part)
      if sc >= 1:
        values(sc - 1, part)
      gated_activations(sc, part)
    transpose_gated(sc)
  for q in range(parts):
    values(n_sub - 1, q)

  @pl.when(e == pl.num_programs(1) - 1)
  def _():
    y = alpha * x1_ref[...] + (1.0 + gt2_ref[0]) * acc_scr[...]
    o_ref[...] = _layernorm(y) * g2_ref[...] + b2_ref[...]


def _peer_dense(h2t, u, v, ca, na, eb, rb, x1, gt2, ln2_g, ln2_b, seq, alpha):
  d, t = h2t.shape
  n_exp = u.shape[0]
  heads, nk, _ = ca.shape
  assert n_exp == nk * nk
  per_step = 16
  tn, ec = 512, per_step * nk
  assert seq % tn == 0 and n_exp % ec == 0 and ec % PEER_SUB_EXPERTS == 0 and PEER_SUB_EXPERTS % nk == 0
  ub, vb = u.astype(BF16), v.astype(BF16)
  tok = lambda width: pl.BlockSpec((tn, width), lambda i, e: (i, 0))
  expert = pl.BlockSpec((ec, d), lambda i, e: (e, 0))
  fac = pl.BlockSpec((heads, nk, tn), lambda i, e: (0, 0, i))
  fac_i = pl.BlockSpec((heads, per_step, tn), lambda i, e: (0, e, i))
  one = pl.BlockSpec((1, d), lambda i, e: (0, 0))
  return pl.pallas_call(
      functools.partial(_peer_dense_kernel, alpha=alpha),
      grid=(t // tn, n_exp // ec),
      in_specs=[pl.BlockSpec((d, tn), lambda i, e: (0, i)), expert, expert, fac_i, fac_i, fac, fac, tok(d),
                pl.BlockSpec((1, 1, d), lambda i, e: ((i * tn) // seq, 0, 0)), one, one],
      out_specs=tok(d),
      out_shape=jax.ShapeDtypeStruct((t, d), F32),
      scratch_shapes=[pltpu.VMEM((tn, d), F32), pltpu.VMEM((2, PEER_SUB_EXPERTS, tn), F32),
                      pltpu.VMEM((2, PEER_SUB_EXPERTS, tn), BF16), pltpu.VMEM((2, tn, PEER_SUB_EXPERTS), BF16)],
      compiler_params=_params("arbitrary", "arbitrary"),
      name="peer_dense",
  )(h2t, ub, vb, ca, na, eb, rb, x1, gt2, ln2_g, ln2_b)


def kernel(x, c, w_cond, b_cond, w_in, w_attn_up, w_gmlp_up, w_out, sgu_ln_g, sgu_ln_b, w_spatial, b_spatial,
           ln1_g, ln1_b, w_peer_q, peer_sub_keys, peer_u, peer_v, ln2_g, ln2_b):
  bsz, s, d = x.shape
  t = bsz * s
  depth = w_cond.shape[0]
  alpha = float((2.0 * depth) ** 0.25)
  for l in range(depth):
    mod = _cond(c, w_cond[l], b_cond[l])
    sh1, sc1, gt1, sh2, sc2, gt2 = [m[:, None, :] for m in jnp.split(mod, 6, axis=-1)]
    q, k, v, kmean, gu, gv, ga, gg = _in_proj(x, sh1, sc1, w_in[l])
    attn = _moba_attention(q, k, v, kmean)
    sgu = _sgu(gu.reshape(t, -1), gv.reshape(t, -1), sgu_ln_g[l], sgu_ln_b[l], w_spatial[l], b_spatial[l])
    x1, h2t, st = _merge_out(attn, sgu.reshape(bsz, s, -1), ga, gg, x, gt1, sh2, sc2,
                             ln1_g[l].reshape(1, d), ln1_b[l].reshape(1, d),
                             w_attn_up[l], w_gmlp_up[l], w_out[l], w_peer_q[l], peer_sub_keys[l], alpha)
    ca, na, eb, rb = _peer_route(st)
    x = _peer_dense(h2t, peer_u[l], peer_v[l], ca, na, eb, rb, x1.reshape(t, d), gt2,
                    ln2_g[l].reshape(1, d), ln2_b[l].reshape(1, d), s, alpha).reshape(bsz, s, d)
  return x
```

```python
import functools

import jax
import jax.numpy as jnp
from jax import lax
from jax.experimental import pallas as pl
from jax.experimental.pallas import tpu as pltpu

F32 = jnp.float32
BF16 = jnp.bfloat16
HIGHEST = lax.Precision.HIGHEST

N_ATTN_HEADS = 8
ATTN_HEAD_DIM = 64
MOBA_BLOCK = 256
MOBA_TOPK = 3
GMLP_GROUPS = 8
GMLP_GROUP_DIM = 64
GMLP_CHUNK = 128
PEER_HEADS = 8
PEER_N_KEYS = 128
PEER_TOPK = 16
LN_EPS = 1e-5

LANES = 128
V7X_VMEM_LIMIT_BYTES = 56 * 1024 * 1024

NEG_BIG = -1e30
LOG2E = 1.4426950408889634


def _params(*sem):
  return pltpu.CompilerParams(dimension_semantics=sem, vmem_limit_bytes=V7X_VMEM_LIMIT_BYTES)


def _layernorm(x):
  mu = jnp.mean(x, axis=-1, keepdims=True)
  d = x - mu
  var = jnp.mean(d * d, axis=-1, keepdims=True)
  return d * lax.rsqrt(var + LN_EPS)


def _gelu(x):
  return 0.5 * x * (1.0 + lax.erf(x * (2.0 ** -0.5)))


def _cond_kernel(c_ref, w_ref, b_ref, o_ref):
  cs = jax.nn.silu(c_ref[...])
  o_ref[...] = jnp.dot(cs, w_ref[...], preferred_element_type=F32, precision=HIGHEST) + b_ref[...]


def _cond(c, w, b):
  bsz, d = c.shape
  n = w.shape[1]
  rows = 8
  cp = jnp.zeros((rows, d), F32).at[:bsz].set(c)
  tn = 1536 if n % 1536 == 0 else n
  out = pl.pallas_call(
      _cond_kernel,
      grid=(n // tn,),
      in_specs=[
          pl.BlockSpec((rows, d), lambda j: (0, 0)),
          pl.BlockSpec((d, tn), lambda j: (0, j)),
          pl.BlockSpec((1, tn), lambda j: (0, j)),
      ],
      out_specs=pl.BlockSpec((rows, tn), lambda j: (0, j)),
      out_shape=jax.ShapeDtypeStruct((rows, n), F32),
      compiler_params=_params("arbitrary"),
      name="adaln_cond",
  )(cp, w, b.reshape(1, n))
  return out[:bsz]


def _inproj_kernel(x_ref, sh_ref, sc_ref, wq, wk, wv, wgu, wgv, wga, wgg,
                   q_ref, k_ref, v_ref, km_ref, gu_ref, gv_ref, ga_ref, gg_ref):
  h = (_layernorm(x_ref[0]) * (1.0 + sc_ref[0]) + sh_ref[0]).astype(BF16)
  dot = lambda w: jnp.dot(h, w[...], preferred_element_type=F32)
  q_ref[0] = dot(wq).T
  k = dot(wk)
  tm, width = k.shape
  km_ref[0, 0] = jnp.mean(k.reshape(tm // MOBA_BLOCK, MOBA_BLOCK, width), axis=1)
  k_ref[0] = k.astype(BF16)
  v = dot(wv)
  for kb in range(tm // MOBA_BLOCK):
    v_ref[0, kb] = v[kb * MOBA_BLOCK:(kb + 1) * MOBA_BLOCK].T.astype(BF16)
  gu_ref[0] = dot(wgu)
  gv_ref[0] = dot(wgv)
  ga_ref[0] = dot(wga)
  gg_ref[0] = dot(wgg)


def _in_proj(x, sh, sc, w_in):
  bsz, s, d = x.shape
  aw = N_ATTN_HEADS * ATTN_HEAD_DIM
  gw = GMLP_GROUPS * GMLP_GROUP_DIM
  bounds = [0, aw, 2 * aw, 3 * aw, 3 * aw + gw, 3 * aw + 2 * gw, 3 * aw + 2 * gw + d, 3 * aw + 2 * gw + 2 * d]
  assert w_in.shape[1] == bounds[-1]
  ws = [w_in[:, a:b].astype(BF16) for a, b in zip(bounds[:-1], bounds[1:])]
  tm = 512
  assert s % tm == 0 and tm % MOBA_BLOCK == 0
  nt = s // tm
  row = lambda width: pl.BlockSpec((1, tm, width), lambda b, i: (b, i, 0))
  vec = pl.BlockSpec((1, 1, d), lambda b, i: (b, 0, 0))
  wspec = lambda w: pl.BlockSpec(w.shape, lambda b, i: (0, 0))
  sds = lambda width, dt: jax.ShapeDtypeStruct((bsz, s, width), dt)
  kpt = tm // MOBA_BLOCK
  outs = pl.pallas_call(
      _inproj_kernel,
      grid=(bsz, nt),
      in_specs=[row(d), vec, vec] + [wspec(w) for w in ws],
      out_specs=[pl.BlockSpec((1, aw, tm), lambda b, i: (b, 0, i)), row(aw),
                 pl.BlockSpec((1, kpt, aw, MOBA_BLOCK), lambda b, i: (b, i, 0, 0)),
                 pl.BlockSpec((1, 1, kpt, aw), lambda b, i: (b, i, 0, 0)),
                 row(gw), row(gw), row(d), row(d)],
      out_shape=[jax.ShapeDtypeStruct((bsz, aw, s), F32), sds(aw, BF16),
                 jax.ShapeDtypeStruct((bsz, s // MOBA_BLOCK, aw, MOBA_BLOCK), BF16),
                 jax.ShapeDtypeStruct((bsz, nt, kpt, aw), F32),
                 sds(gw, F32), sds(gw, F32), sds(d, F32), sds(d, F32)],
      compiler_params=_params("arbitrary", "arbitrary"),
      name="in_proj",
  )(x, sh, sc, *ws)
  q, k, v, km, gu, gv, ga, gg = outs
  return q, k, v, km.reshape(bsz, s // MOBA_BLOCK, aw), gu, gv, ga, gg


def _alibi_slopes():
  return [2.0 ** (-8.0 * (i + 1) / N_ATTN_HEADS) for i in range(N_ATTN_HEADS)]


def _key_side_table(s):
  nb = s // MOBA_BLOCK
  assert nb + 6 <= LANES
  pos = jnp.arange(s, dtype=F32)
  blk = jnp.arange(s, dtype=jnp.int32) // MOBA_BLOCK
  onehot = (blk[:, None] == jnp.arange(nb)[None, :]).astype(F32)
  slopes = _alibi_slopes()

  def top16(a):
    bits = lax.bitcast_convert_type(a, jnp.uint32) & jnp.uint32(0xFFFF0000)
    return lax.bitcast_convert_type(bits, F32)

  tables = []
  for hp in range(N_ATTN_HEADS // 2):
    cols = [onehot]
    for h in (2 * hp, 2 * hp + 1):
      c = (slopes[h] * LOG2E) * pos
      hi = top16(c)
      mid = top16(c - hi)
      lo = c - hi - mid
      cols += [hi[:, None], mid[:, None], lo[:, None]]
    t = jnp.concatenate(cols, axis=1)
    tables.append(jnp.pad(t, ((0, 0), (0, LANES - t.shape[1]))))
  return jnp.stack(tables).astype(BF16)


ATTN_BLOCKS_PER_STEP = 4


def _attn_kernel(qt_ref, k_ref, kb_ref, vt_ref, km_ref, o_ref, m_scr, l_scr, acc_scr, sa_scr, sb_scr, *, nb):
  j = pl.program_id(2)
  tq = MOBA_BLOCK
  hd = ATTN_HEAD_DIM
  q2 = qt_ref[0]
  row = lax.broadcasted_iota(jnp.int32, (LANES, tq), 0)
  qscale = (hd ** -0.5) * LOG2E

  rhs_own, rhs_past = [], []
  for h in range(2):
    qh = jnp.where((row >= hd * h) & (row < hd * (h + 1)), q2, 0.0)
    gate = jnp.dot(km_ref[0], qh, preferred_element_type=F32, precision=HIGHEST)
    blk = lax.broadcasted_iota(jnp.int32, (nb, tq), 0)
    g = jnp.where(blk < j, gate, -jnp.inf)
    sel = blk < 0
    for _ in range(MOBA_TOPK):
      mx = jnp.max(g, axis=0, keepdims=True)
      idx = jnp.min(jnp.where(g == mx, blk, nb), axis=0, keepdims=True)
      pick = (blk == idx) & (g > -jnp.inf)
      sel = sel | pick
      g = jnp.where(pick, -jnp.inf, g)
    rest = lax.broadcasted_iota(jnp.int32, (LANES - nb, tq), 0)
    tail = jnp.where((rest >= 3 * h) & (rest < 3 * h + 3), 1.0, 0.0)
    qb = (qh * qscale).astype(BF16)
    side_past = jnp.concatenate([jnp.where(sel, 0.0, NEG_BIG), tail], axis=0).astype(BF16)
    side_own = jnp.concatenate([jnp.where(blk == j, 0.0, NEG_BIG), tail], axis=0).astype(BF16)
    rhs_past.append(jnp.concatenate([qb, side_past], axis=0))
    rhs_own.append(jnp.concatenate([qb, side_own], axis=0))

  def keys(off, n):
    return jnp.concatenate([k_ref[0, pl.ds(off, n), :], kb_ref[0, pl.ds(off, n), :]], axis=1)

  own = pl.multiple_of(j * tq, tq)
  kr = lax.broadcasted_iota(jnp.int32, (tq, tq), 0)
  qc = lax.broadcasted_iota(jnp.int32, (tq, tq), 1)
  k_own = keys(own, tq)
  for h in range(2):
    s = jnp.where(kr <= qc, jnp.dot(k_own, rhs_own[h], preferred_element_type=F32), -jnp.inf)
    m = jnp.max(s, axis=0, keepdims=True)
    p = jnp.exp2(s - m)
    m_scr[h] = m
    l_scr[h] = jnp.sum(p, axis=0, keepdims=True)
    acc_scr[h] = jnp.dot(vt_ref[0, j, hd * h:hd * (h + 1), :], p.astype(BF16), preferred_element_type=F32)

  per = ATTN_BLOCKS_PER_STEP
  n_chunks = (j + per - 1) // per
  c_last = nb // per - 1

  def qk(c, s_ref):
    kk = keys(pl.multiple_of(c * (per * tq), per * tq), per * tq)
    for h in range(2):
      s_ref[h] = jnp.dot(kk, rhs_past[h], preferred_element_type=F32)

  def softmax_pv(c, s_ref):
    for h in range(2):
      s = s_ref[h]
      m_old = m_scr[h]
      m_new = jnp.maximum(m_old, jnp.max(s, axis=0, keepdims=True))
      alpha = jnp.exp2(m_old - m_new)
      p = jnp.exp2(s - m_new)
      m_scr[h] = m_new
      l_scr[h] = alpha * l_scr[h] + jnp.sum(p, axis=0, keepdims=True)
      p = p.astype(BF16)
      pv = sum(jnp.dot(vt_ref[0, c * per + i, hd * h:hd * (h + 1), :], p[i * tq:(i + 1) * tq],
                       preferred_element_type=F32) for i in range(per))
      acc_scr[h] = alpha * acc_scr[h] + pv

  @pl.when(n_chunks > 0)
  def _():
    qk(0, sa_scr)

  def pair(i, carry):
    qk(2 * i + 1, sb_scr)
    softmax_pv(2 * i, sa_scr)
    qk(jnp.minimum(2 * i + 2, c_last), sa_scr)
    softmax_pv(2 * i + 1, sb_scr)
    return carry

  lax.fori_loop(0, (n_chunks + 1) // 2, pair, 0)
  out_t = jnp.concatenate([acc_scr[0] / l_scr[0], acc_scr[1] / l_scr[1]], axis=0)
  o_ref[0] = out_t.T.astype(o_ref.dtype)


def _moba_attention(qt, k, vt, kmean):
  bsz, aw, s = qt.shape
  nb = s // MOBA_BLOCK
  per = ATTN_BLOCKS_PER_STEP
  assert s % MOBA_BLOCK == 0 and 2 * ATTN_HEAD_DIM == LANES and nb % (2 * per) == 0
  kb = _key_side_table(s)
  tq = MOBA_BLOCK
  return pl.pallas_call(
      functools.partial(_attn_kernel, nb=nb),
      grid=(bsz, N_ATTN_HEADS // 2, nb),
      in_specs=[
          pl.BlockSpec((1, LANES, tq), lambda b, hp, j: (b, hp, j)),
          pl.BlockSpec((1, s, LANES), lambda b, hp, j: (b, 0, hp)),
          pl.BlockSpec((1, s, LANES), lambda b, hp, j: (hp, 0, 0)),
          pl.BlockSpec((1, nb, LANES, tq), lambda b, hp, j: (b, 0, hp, 0)),
          pl.BlockSpec((1, nb, LANES), lambda b, hp, j: (b, 0, hp)),
      ],
      out_specs=pl.BlockSpec((1, tq, LANES), lambda b, hp, j: (b, j, hp)),
      out_shape=jax.ShapeDtypeStruct((bsz, s, aw), BF16),
      scratch_shapes=[pltpu.VMEM((2, 1, tq), F32), pltpu.VMEM((2, 1, tq), F32),
                      pltpu.VMEM((2, ATTN_HEAD_DIM, tq), F32),
                      pltpu.VMEM((2, per * tq, tq), F32), pltpu.VMEM((2, per * tq, tq), F32)],
      compiler_params=_params("arbitrary", "arbitrary", "arbitrary"),
      name="moba_attn",
  )(qt, k, kb, vt, kmean)


def _split2(x):
  hi = x.astype(BF16)
  lo = (x - hi.astype(F32)).astype(BF16)
  return hi, lo


def _sgu_kernel(gu_ref, gv_ref, avg_ref, lng_ref, lnb_ref, wsp_ref, bsp_ref, o_ref):
  tm, width = gu_ref.shape
  avg = avg_ref[...]

  def group_mean(a):
    return sum(jnp.dot(p, avg, preferred_element_type=F32) for p in _split2(a))

  v = _gelu(gv_ref[...])
  d = v - group_mean(v)
  vv = (d * lax.rsqrt(group_mean(d * d) + LN_EPS) * lng_ref[...] + lnb_ref[...]).astype(BF16)
  u = _gelu(gu_ref[...])

  rows = lax.broadcasted_iota(jnp.int32, (GMLP_CHUNK, GMLP_CHUNK), 0)
  cols = lax.broadcasted_iota(jnp.int32, (GMLP_CHUNK, GMLP_CHUNK), 1)
  grp = lax.broadcasted_iota(jnp.int32, (GMLP_CHUNK, width), 1) // GMLP_GROUP_DIM
  ws = [jnp.where(rows >= cols, wsp_ref[g], 0.0).astype(BF16) for g in range(GMLP_GROUPS)]
  for ck in range(tm // GMLP_CHUNK):
    sl = slice(ck * GMLP_CHUNK, (ck + 1) * GMLP_CHUNK)
    vc = vv[sl]
    sv = bsp_ref[...]
    for g in range(GMLP_GROUPS):
      sv = sv + jnp.where(grp == g, jnp.dot(ws[g], vc, preferred_element_type=F32), 0.0)
    o_ref[sl, :] = (u[sl] * sv).astype(o_ref.dtype)


def _sgu(gu, gv, ln_g, ln_b, w_spatial, b_spatial):
  t, width = gu.shape
  tm = 512
  assert t % tm == 0 and tm % GMLP_CHUNK == 0 and width == GMLP_GROUPS * GMLP_GROUP_DIM
  ch = jnp.arange(width) // GMLP_GROUP_DIM
  avg = ((ch[:, None] == ch[None, :]).astype(F32) / GMLP_GROUP_DIM).astype(BF16)
  bsp = jnp.repeat(b_spatial.T, GMLP_GROUP_DIM, axis=1)
  row = pl.BlockSpec((tm, width), lambda i: (i, 0))
  full = lambda a: pl.BlockSpec(a.shape, lambda i: (0,) * a.ndim)
  args = (gu, gv, avg, ln_g.reshape(1, width), ln_b.reshape(1, width), w_spatial, bsp)
  return pl.pallas_call(
      _sgu_kernel,
      grid=(t // tm,),
      in_specs=[row, row] + [full(a) for a in args[2:]],
      out_specs=row,
      out_shape=jax.ShapeDtypeStruct((t, width), BF16),
      compiler_params=_params("arbitrary"),
      name="sgu",
  )(*args)


def _merge_kernel(attn_ref, sgu_ref, ga_ref, gg_ref, x_ref, gt1_ref, sh2_ref, sc2_ref, g1_ref, b1_ref,
                  wa_ref, wg_ref, wo_ref, wq_ref, keys_ref, x1_ref, h2t_ref, st_ref, *, alpha):
  a = jnp.dot(attn_ref[0], wa_ref[...], preferred_element_type=F32)
  g = jnp.dot(sgu_ref[0], wg_ref[...], preferred_element_type=F32)
  merged = jax.nn.sigmoid(ga_ref[0]) * a + jax.nn.sigmoid(gg_ref[0]) * g
  out = jnp.dot(merged.astype(BF16), wo_ref[...], preferred_element_type=F32)
  x1 = _layernorm(alpha * x_ref[0] + (1.0 + gt1_ref[0]) * out) * g1_ref[...] + b1_ref[...]
  x1_ref[0] = x1
  h2f = _layernorm(x1) * (1.0 + sc2_ref[0]) + sh2_ref[0]
  h2t_ref[...] = h2f.T.astype(BF16)
  h2 = h2f.astype(BF16)
  qp = jnp.dot(h2, wq_ref[...], preferred_element_type=F32).astype(BF16)
  half = keys_ref.shape[2]
  for hc in range(keys_ref.shape[0]):
    st_ref[hc] = lax.dot_general(keys_ref[hc], qp[:, hc * half:(hc + 1) * half],
                                 (((1,), (1,)), ((), ())), preferred_element_type=F32)


def _merge_out(attn, sgu, ga, gg, x, gt1, sh2, sc2, ln1_g, ln1_b, w_attn_up, w_gmlp_up, w_out, w_peer_q,
               sub_keys, alpha):
  bsz, s, d = x.shape
  tm = 256
  assert s % tm == 0
  nt = s // tm
  hc, nk, half = sub_keys.shape[0] * sub_keys.shape[1], sub_keys.shape[2], sub_keys.shape[3]
  keys = sub_keys.reshape(hc, nk, half).astype(BF16)
  ws = [w.astype(BF16) for w in (w_attn_up, w_gmlp_up, w_out, w_peer_q)]
  assert w_peer_q.shape[1] == hc * half
  row = lambda width: pl.BlockSpec((1, tm, width), lambda b, i: (b, i, 0))
  vec = pl.BlockSpec((1, 1, d), lambda b, i: (b, 0, 0))
  full = lambda a: pl.BlockSpec(a.shape, lambda b, i: (0,) * a.ndim)
  aw = attn.shape[-1]
  return pl.pallas_call(
      functools.partial(_merge_kernel, alpha=alpha),
      grid=(bsz, nt),
      in_specs=[row(aw), row(sgu.shape[-1]), row(d), row(d), row(d), vec, vec, vec,
                full(ln1_g), full(ln1_b)] + [full(w) for w in ws] + [full(keys)],
      out_specs=[row(d), pl.BlockSpec((d, tm), lambda b, i: (0, b * nt + i)),
                 pl.BlockSpec((hc, nk, tm), lambda b, i: (0, 0, b * nt + i))],
      out_shape=[jax.ShapeDtypeStruct((bsz, s, d), F32), jax.ShapeDtypeStruct((d, bsz * s), BF16),
                 jax.ShapeDtypeStruct((hc, nk, bsz * s), F32)],
      compiler_params=_params("arbitrary", "arbitrary"),
      name="merge_out",
  )(attn, sgu, ga, gg, x, gt1, sh2, sc2, ln1_g, ln1_b, *ws, keys)


def _young_cells():
  return [(a, b) for a in range(PEER_TOPK) for b in range(PEER_TOPK) if (a + 1) * (b + 1) <= PEER_TOPK]


def _route_kernel(st_ref, ca_ref, na_ref, eb_ref, rb_ref, tv_scr, cell_scr, cell2_scr):
  nk, tn = st_ref.shape[1], st_ref.shape[2]
  row = lax.broadcasted_iota(jnp.int32, (nk, tn), 0)
  cells = _young_cells()
  npad = cell_scr.shape[0]
  crow = lax.broadcasted_iota(jnp.int32, (npad, tn), 0)
  kf = float(PEER_TOPK)

  def top_sorted(s, half):
    rank = jnp.full((nk, tn), float(nk), F32)
    for r in range(PEER_TOPK):
      mx = jnp.max(s, axis=0, keepdims=True)
      idx = jnp.min(jnp.where(s == mx, row, nk), axis=0, keepdims=True)
      pick = row == idx
      rank = jnp.where(pick, float(r), rank)
      s = jnp.where(pick, -jnp.inf, s)
      tv_scr[half, r:r + 1, :] = mx
    return rank

  def top_values(s, n):
    vals = []
    for _ in range(n):
      mx = jnp.max(s, axis=0, keepdims=True)
      vals.append(mx)
      s = jnp.where(s == mx, -jnp.inf, s)
    return vals

  def count_ge(s, thr):
    return jnp.sum(jnp.where(s >= thr, 1.0, 0.0), axis=0, keepdims=True)

  def sorted_top(s):
    k = PEER_TOPK
    v = [s[8 * i:8 * (i + 1)] for i in range(k)]

    def order(i, l, descending):
      hi, lo = jnp.maximum(v[i], v[l]), jnp.minimum(v[i], v[l])
      v[i], v[l] = (hi, lo) if descending else (lo, hi)

    def merge(span, direction_bit):
      j = span // 2
      while j >= 1:
        for i in range(k):
          if i ^ j > i:
            order(i, i ^ j, (i & direction_bit) == 0)
        j //= 2

    span = 2
    while span <= k:
      merge(span, span)
      span *= 2
    for shift in (7, 6, 4):
      other = [pltpu.roll(v[k - 1 - i], shift, 0) for i in range(k)]
      for i in range(k):
        v[i] = jnp.maximum(v[i], other[i])
      merge(k, k)
    return [v[i][0:1, :] for i in range(k)]

  def distinct(vals):
    ok = vals[0] > vals[1]
    for r in range(1, len(vals) - 1):
      ok = ok & (vals[r] > vals[r + 1])
    return ok

  def head_no_ties(h, cell_scr):
    n = PEER_TOPK
    s0, s1 = st_ref[2 * h], st_ref[2 * h + 1]
    tv0, tv1 = sorted_top(s0), sorted_top(s1)
    ok = (count_ge(s0, tv0[-1]) == float(n)) & (count_ge(s1, tv1[-1]) == float(n)) & distinct(tv0) & distinct(tv1)
    for ci, (a, b) in enumerate(cells):
      cell_scr[ci:ci + 1, :] = tv0[a] + tv1[b]
    if npad > len(cells):
      cell_scr[len(cells):, :] = jnp.full((npad - len(cells), tn), -jnp.inf, F32)
    cand = cell_scr[...]
    cv = top_values(cand, n)
    ok = ok & (count_ge(cand, cv[-1]) == float(n))
    z = jnp.ones((1, tn), F32)
    for r in range(1, PEER_TOPK):
      z = z + jnp.exp(cv[r] - cv[0])
    cell_scr[...] = jnp.where(cand >= cv[PEER_TOPK - 1], 1.0, 0.0)
    na = jnp.zeros((nk, tn), F32)
    start = 0
    for a in range(PEER_TOPK):
      cnt = sum(1 for (aa, _) in cells if aa == a)
      n_a = jnp.sum(cell_scr[start:start + cnt, :], axis=0, keepdims=True)
      na = jnp.where(s0 == tv0[a], n_a, na)
      start += cnt
    assert PEER_TOPK == 16
    c8 = tv1[7] > s1
    c4 = jnp.where(c8, tv1[11], tv1[3]) > s1
    c2 = jnp.where(c8, jnp.where(c4, tv1[13], tv1[9]), jnp.where(c4, tv1[5], tv1[1])) > s1
    c1 = jnp.where(c8, jnp.where(c4, jnp.where(c2, tv1[14], tv1[12]), jnp.where(c2, tv1[10], tv1[8])),
                   jnp.where(c4, jnp.where(c2, tv1[6], tv1[4]), jnp.where(c2, tv1[2], tv1[0]))) > s1
    rb = (jnp.where(c8, 8.0, 0.0) + jnp.where(c4, 4.0, 0.0) + jnp.where(c2, 2.0, 0.0) + jnp.where(c1, 1.0, 0.0)
          + jnp.where(tv1[15] > s1, 1.0, 0.0))
    ca_ref[h] = jnp.where(s0 >= tv0[PEER_TOPK - 1], 0.5 * jnp.exp(s0 - tv0[0]) / z, 0.0)
    na_ref[h] = na
    eb_ref[h] = jnp.exp(s1 - tv1[0]).astype(eb_ref.dtype)
    rb_ref[h] = rb.astype(rb_ref.dtype)
    okf = jnp.broadcast_to(jnp.where(ok, 1.0, 0.0), (8, tn))
    return jnp.min(jnp.min(okf, axis=1, keepdims=True), axis=0, keepdims=True)[0, 0] > 0.5

  def head_pair(p, carry):
    no_ties = [head_no_ties(2 * p + q, (cell_scr, cell2_scr)[q]) for q in range(2)]
    for q in range(2):
      @pl.when(jnp.logical_not(no_ties[q]))
      def _():
        head_exact(2 * p + q)

    return carry

  def head_exact(h):
    rank0 = top_sorted(st_ref[2 * h], 0)
    rank1 = top_sorted(st_ref[2 * h + 1], 1)
    tv0 = tv_scr[0]
    tv1 = tv_scr[1]
    for ci, (a, b) in enumerate(cells):
      cell_scr[ci:ci + 1, :] = tv0[a:a + 1] + tv1[b:b + 1]
    if npad > len(cells):
      cell_scr[len(cells):, :] = jnp.full((npad - len(cells), tn), -jnp.inf, F32)
    cand = cell_scr[...]
    top = tv0[0:1] + tv1[0:1]
    picked = jnp.zeros((npad, tn), F32)
    z = jnp.zeros((1, tn), F32)
    for _ in range(PEER_TOPK):
      mx = jnp.max(cand, axis=0, keepdims=True)
      idx = jnp.min(jnp.where(cand == mx, crow, npad), axis=0, keepdims=True)
      pick = crow == idx
      picked = jnp.where(pick, 1.0, picked)
      cand = jnp.where(pick, -jnp.inf, cand)
      z = z + jnp.exp(mx - top)
    cell_scr[...] = picked
    na = jnp.zeros((nk, tn), F32)
    start = 0
    for a in range(PEER_TOPK):
      cnt = sum(1 for (aa, _) in cells if aa == a)
      n_a = jnp.sum(cell_scr[start:start + cnt, :], axis=0, keepdims=True)
      na = jnp.where(rank0 == float(a), n_a, na)
      start += cnt
    ca_ref[h] = jnp.where(rank0 < kf, 0.5 * jnp.exp(st_ref[2 * h] - tv0[0:1]) / z, 0.0)
    na_ref[h] = na
    eb_ref[h] = jnp.exp(st_ref[2 * h + 1] - tv1[0:1]).astype(eb_ref.dtype)
    rb_ref[h] = rank1.astype(rb_ref.dtype)

  lax.fori_loop(0, ca_ref.shape[0] // 2, head_pair, 0)


def _peer_route(st):
  hc, nk, t = st.shape
  heads = hc // 2
  tn = LANES
  assert t % tn == 0
  npad = -(-len(_young_cells()) // 8) * 8
  out = lambda dt: jax.ShapeDtypeStruct((heads, nk, t), dt)
  ospec = pl.BlockSpec((heads, nk, tn), lambda i: (0, 0, i))
  return pl.pallas_call(
      _route_kernel,
      grid=(t // tn,),
      in_specs=[pl.BlockSpec((hc, nk, tn), lambda i: (0, 0, i))],
      out_specs=[ospec] * 4,
      out_shape=[out(F32), out(F32), out(BF16), out(BF16)],
      scratch_shapes=[pltpu.VMEM((2, PEER_TOPK, tn), F32), pltpu.VMEM((npad, tn), F32), pltpu.VMEM((npad, tn), F32)],
      compiler_params=_params("arbitrary"),
      name="peer_route",
  )(st)


PEER_SUB_EXPERTS = 512
PEER_UNIT_PIECES = 1

BF16_ROWS = 16


def _peer_dense_kernel(h2t_ref, u_ref, v_ref, ca_ref, na_ref, eb_ref, rb_ref, x1_ref, gt2_ref, g2_ref, b2_ref,
                       o_ref, acc_scr, at_scr, p_scr, pt_scr, *, alpha):
  e = pl.program_id(1)
  ec = u_ref.shape[0]
  tn = h2t_ref.shape[1]
  heads, nk = eb_ref.shape[0], eb_ref.shape[1]
  sub = PEER_SUB_EXPERTS
  n_sub = ec // sub

  @pl.when(e == 0)
  def _():
    acc_scr[...] = jnp.zeros_like(acc_scr)

  parts = PEER_UNIT_PIECES
  piece = sub // parts
  d_model = v_ref.shape[1]
  qcols = d_model // parts

  def pre_activations(sc, part):
    rows = slice(part * piece, (part + 1) * piece)
    at_scr[sc % 2, rows, :] = jnp.dot(u_ref[sc * sub + part * piece:sc * sub + (part + 1) * piece, :], h2t_ref[...],
                                      preferred_element_type=F32)

  def packed_rows(ref, h, r, ls):
    return jnp.broadcast_to(ref[h, r:r + 1, ls], (BF16_ROWS, LANES)).astype(BF16)

  zero = jnp.zeros((), BF16)

  def gate(ca, na, rb, eb):
    return ca * jnp.minimum(jnp.maximum(na - rb, zero), eb)

  def gated_activations(sc, part):
    slot = sc % 2
    for ii in range(piece // nk):
      i_local = part * (piece // nk) + ii
      r = sc * (sub // nk) + i_local
      for lc in range(tn // LANES):
        ls = slice(lc * LANES, (lc + 1) * LANES)
        ca = [packed_rows(ca_ref, h, r, ls) for h in range(heads)]
        na = [packed_rows(na_ref, h, r, ls) for h in range(heads)]
        for rc in range(nk // BF16_ROWS):
          js = slice(rc * BF16_ROWS, (rc + 1) * BF16_ROWS)
          w = gate(ca[0], na[0], rb_ref[0, js, ls], eb_ref[0, js, ls])
          for h in range(1, heads):
            w = w + gate(ca[h], na[h], rb_ref[h, js, ls], eb_ref[h, js, ls])
          rows = slice(i_local * nk + rc * BF16_ROWS, i_local * nk + (rc + 1) * BF16_ROWS)
          a = at_scr[slot, rows, ls]
          p_scr[slot, rows, ls] = w * (a + a * lax.erf(a * (2.0 ** -0.5))).astype(BF16)

  def transpose_gated(sc):
    pt_scr[sc % 2] = p_scr[sc % 2].T

  def values(sc, q):
    cols = slice(q * qcols, (q + 1) * qcols)
    acc_scr[:, cols] += jnp.dot(pt_scr[sc % 2], v_ref[sc * sub:(sc + 1) * sub, cols], preferred_element_type=F32)

  for part in range(parts):
    pre_activations(0, part)
  for sc in range(n_sub):
    for part in range(parts):
      if sc + 1 < n_sub:
        pre_activations(sc + 1, part)
      if sc >= 1:
        values(sc - 1, part)
      gated_activations(sc, part)
    transpose_gated(sc)
  for q in range(parts):
    values(n_sub - 1, q)

  @pl.when(e == pl.num_programs(1) - 1)
  def _():
    y = alpha * x1_ref[...] + (1.0 + gt2_ref[0]) * acc_scr[...]
    o_ref[...] = _layernorm(y) * g2_ref[...] + b2_ref[...]


def _peer_dense(h2t, u, v, ca, na, eb, rb, x1, gt2, ln2_g, ln2_b, seq, alpha):
  d, t = h2t.shape
  n_exp = u.shape[0]
  heads, nk, _ = ca.shape
  assert n_exp == nk * nk
  per_step = 16
  tn, ec = 512, per_step * nk
  assert seq % tn == 0 and n_exp % ec == 0 and ec % PEER_SUB_EXPERTS == 0 and PEER_SUB_EXPERTS % nk == 0
  ub, vb = u.astype(BF16), v.astype(BF16)
  tok = lambda width: pl.BlockSpec((tn, width), lambda i, e: (i, 0))
  expert = pl.BlockSpec((ec, d), lambda i, e: (e, 0))
  fac = pl.BlockSpec((heads, nk, tn), lambda i, e: (0, 0, i))
  fac_i = pl.BlockSpec((heads, per_step, tn), lambda i, e: (0, e, i))
  one = pl.BlockSpec((1, d), lambda i, e: (0, 0))
  return pl.pallas_call(
      functools.partial(_peer_dense_kernel, alpha=alpha),
      grid=(t // tn, n_exp // ec),
      in_specs=[pl.BlockSpec((d, tn), lambda i, e: (0, i)), expert, expert, fac_i, fac_i, fac, fac, tok(d),
                pl.BlockSpec((1, 1, d), lambda i, e: ((i * tn) // seq, 0, 0)), one, one],
      out_specs=tok(d),
      out_shape=jax.ShapeDtypeStruct((t, d), F32),
      scratch_shapes=[pltpu.VMEM((tn, d), F32), pltpu.VMEM((2, PEER_SUB_EXPERTS, tn), F32),
                      pltpu.VMEM((2, PEER_SUB_EXPERTS, tn), BF16), pltpu.VMEM((2, tn, PEER_SUB_EXPERTS), BF16)],
      compiler_params=_params("arbitrary", "arbitrary"),
      name="peer_dense",
  )(h2t, ub, vb, ca, na, eb, rb, x1, gt2, ln2_g, ln2_b)


def kernel(x, c, w_cond, b_cond, w_in, w_attn_up, w_gmlp_up, w_out, sgu_ln_g, sgu_ln_b, w_spatial, b_spatial,
           ln1_g, ln1_b, w_peer_q, peer_sub_keys, peer_u, peer_v, ln2_g, ln2_b):
  bsz, s, d = x.shape
  t = bsz * s
  depth = w_cond.shape[0]
  alpha = float((2.0 * depth) ** 0.25)
  for l in range(depth):
    mod = _cond(c, w_cond[l], b_cond[l])
    sh1, sc1, gt1, sh2, sc2, gt2 = [m[:, None, :] for m in jnp.split(mod, 6, axis=-1)]
    q, k, v, kmean, gu, gv, ga, gg = _in_proj(x, sh1, sc1, w_in[l])
    attn = _moba_attention(q, k, v, kmean)
    sgu = _sgu(gu.reshape(t, -1), gv.reshape(t, -1), sgu_ln_g[l], sgu_ln_b[l], w_spatial[l], b_spatial[l])
    x1, h2t, st = _merge_out(attn, sgu.reshape(bsz, s, -1), ga, gg, x, gt1, sh2, sc2,
                             ln1_g[l].reshape(1, d), ln1_b[l].reshape(1, d),
                             w_attn_up[l], w_gmlp_up[l], w_out[l], w_peer_q[l], peer_sub_keys[l], alpha)
    ca, na, eb, rb = _peer_route(st)
    x = _peer_dense(h2t, peer_u[l], peer_v[l], ca, na, eb, rb, x1.reshape(t, d), gt2,
                    ln2_g[l].reshape(1, d), ln2_b[l].reshape(1, d), s, alpha).reshape(bsz, s, d)
  return x
```
